```python
import math
import jax, jax.numpy as jnp
from jax import lax
import numpy as np

D_MODEL = 1024
BATCH = 2
SEQ = 16384
DEPTH = 2

D_MIX = 2 * D_MODEL
M_HEADS = 4
M_HEAD_DIM = D_MIX // 4 // M_HEADS
M_WIDTH = M_HEADS * M_HEAD_DIM
M_CONV = 4
M_CHUNK = 128
S_WIDTH = D_MIX // 2
S_HEAD_DIM = 64
S_HEADS = S_WIDTH // S_HEAD_DIM
S_GROUPS = 4
S_STATE = 128
S_CONV = 4
S_CHUNK = 128
C_WIDTH = D_MIX - M_WIDTH - S_WIDTH
C_KERNEL = 31
D_FF = ((8 * D_MODEL // 3 + 255) // 256) * 256

RMS_EPS = 1e-6
LN_EPS = 1e-5
M_INIT_LOG = -1e30

IN_SIZES = (M_WIDTH, M_WIDTH, M_WIDTH, M_WIDTH, M_HEADS, M_HEADS,
            S_WIDTH, S_WIDTH + 2 * S_GROUPS * S_STATE, S_HEADS, 2 * C_WIDTH)
D_IN_PROJ = sum(IN_SIZES)

kernel_name = "hybrid_mlstm_ssd_convmodule_macaron"


def rmsnorm(x, g):
    xf = x.astype(jnp.float32)
    y = xf * lax.rsqrt(jnp.mean(xf * xf, axis=-1, keepdims=True) + RMS_EPS)
    return (y * g.astype(jnp.float32)).astype(x.dtype)


def group_rmsnorm(x, g, n_groups):
    shp = x.shape
    xf = x.astype(jnp.float32).reshape(shp[:-1] + (n_groups, shp[-1] // n_groups))
    y = xf * lax.rsqrt(jnp.mean(xf * xf, axis=-1, keepdims=True) + RMS_EPS)
    return (y.reshape(shp) * g.astype(jnp.float32)).astype(x.dtype)


def layernorm(x, g, b):
    xf = x.astype(jnp.float32)
    mu = jnp.mean(xf, axis=-1, keepdims=True)
    xc = xf - mu
    y = xc * lax.rsqrt(jnp.mean(xc * xc, axis=-1, keepdims=True) + LN_EPS)
    return (y * g.astype(jnp.float32) + b.astype(jnp.float32)).astype(x.dtype)


def causal_dwconv(x, w, b):
    width = w.shape[0]
    y = lax.conv_general_dilated(
        x, w[:, None, :].astype(x.dtype), window_strides=(1,),
        padding=[(width - 1, 0)], dimension_numbers=("NWC", "WIO", "NWC"),
        feature_group_count=x.shape[-1])
    return y + b.astype(x.dtype)


def swiglu(h, wg, wu, wd):
    return (jax.nn.silu(h @ wg) * (h @ wu)) @ wd


def split_cols(a, sizes):
    out, off = [], 0
    for s in sizes:
        out.append(a[..., off:off + s])
        off += s
    return out


def mlstm_chunkwise(q, k, v, i_pre, f_pre):
    bsz, s, nh, dh = q.shape
    L = M_CHUNK
    nc = s // L
    f32 = jnp.float32

    def chunks(a):
        return a.astype(f32).reshape(bsz, nc, L, nh, -1).transpose(1, 0, 3, 2, 4)

    qc = chunks(q)
    kc = chunks(k) * (dh ** -0.5)
    vc = chunks(v)
    ic = chunks(i_pre[..., None])[..., 0]
    fc = chunks(jax.nn.log_sigmoid(f_pre.astype(f32))[..., None])[..., 0]
    causal = jnp.tril(jnp.ones((L, L), dtype=bool))

    def step(carry, inp):
        C, n, m = carry
        qb, kb, vb, ib, fb = inp
        b = jnp.cumsum(fb, axis=-1)
        g = b[..., -1]
        dmat = jnp.where(causal, b[..., :, None] - b[..., None, :] + ib[..., None, :], -jnp.inf)
        inter = b + m[..., None]
        m_t = jnp.maximum(inter, jnp.max(dmat, axis=-1))
        w = jnp.exp(dmat - m_t[..., None])
        sqk = jnp.einsum("bhtd,bhsd->bhts", qb, kb) * w
        a = jnp.exp(inter - m_t)
        num = jnp.einsum("bhts,bhse->bhte", sqk, vb) + a[..., None] * jnp.einsum("bhed,bhtd->bhte", C, qb)
        den = jnp.sum(sqk, axis=-1) + a * jnp.einsum("bhd,bhtd->bht", n, qb)
        h = num / jnp.maximum(jnp.abs(den), jnp.exp(-m_t))[..., None]
        u = g[..., None] - b + ib
        m_new = jnp.maximum(g + m, jnp.max(u, axis=-1))
        ws = jnp.exp(u - m_new[..., None])
        decay = jnp.exp(g + m - m_new)
        C_new = decay[..., None, None] * C + jnp.einsum("bhs,bhse,bhsd->bhed", ws, vb, kb)
        n_new = decay[..., None] * n + jnp.einsum("bhs,bhsd->bhd", ws, kb)
        return (C_new, n_new, m_new), h

    init = (jnp.zeros((bsz, nh, dh, dh), f32), jnp.zeros((bsz, nh, dh), f32),
            jnp.full((bsz, nh), M_INIT_LOG, f32))
    _, hs = lax.scan(step, init, (qc, kc, vc, ic, fc))
    return hs.transpose(1, 0, 3, 2, 4).reshape(bsz, s, nh, dh)


def ssd_chunkwise(xs, dt, A, Bm, Cm):
    bsz, s, nh, hp = xs.shape
    ng, ns = Bm.shape[2], Bm.shape[3]
    r = nh // ng
    L = S_CHUNK
    nc = s // L
    f32 = jnp.float32
    X = xs.astype(f32) * dt[..., None]
    adt = dt * A
    xc = X.reshape(bsz, nc, L, ng, r, hp).transpose(1, 0, 3, 4, 2, 5)
    ac = adt.reshape(bsz, nc, L, ng, r).transpose(1, 0, 3, 4, 2)
    bc = Bm.astype(f32).reshape(bsz, nc, L, ng, ns).transpose(1, 0, 3, 2, 4)
    cc = Cm.astype(f32).reshape(bsz, nc, L, ng, ns).transpose(1, 0, 3, 2, 4)
    causal = jnp.tril(jnp.ones((L, L), dtype=bool))

    def step(hstate, inp):
        xb, ab, bb, cb = inp
        cs = jnp.cumsum(ab, axis=-1)
        lmat = jnp.exp(jnp.where(causal, cs[..., :, None] - cs[..., None, :], -jnp.inf))
        cbm = jnp.einsum("bgtn,bgsn->bgts", cb, bb)
        y_diag = jnp.einsum("bgrts,bgrsp->bgrtp", cbm[:, :, None] * lmat, xb)
        y_off = jnp.einsum("bgtn,bgrpn->bgrtp", cb, hstate) * jnp.exp(cs)[..., None]
        decay_s = jnp.exp(cs[..., -1:] - cs)
        h_new = jnp.exp(cs[..., -1])[..., None, None] * hstate + \
            jnp.einsum("bgsn,bgrs,bgrsp->bgrpn", bb, decay_s, xb)
        return h_new, y_diag + y_off

    init = jnp.zeros((bsz, ng, r, hp, ns), f32)
    _, ys = lax.scan(step, init, (xc, ac, bc, cc))
    return ys.transpose(1, 0, 4, 2, 3, 5).reshape(bsz, s, nh, hp)


def hybrid_mixer(h, w_in, w_out, mlstm_conv_w, mlstm_conv_b, mlstm_gate_b, mlstm_norm,
                 ssd_conv_w, ssd_conv_b, ssd_dt_bias, ssd_a_log, ssd_d, ssd_norm,
                 cm_conv_w, cm_conv_b, cm_ln_g, cm_ln_b):
    bsz, s, _ = h.shape
    f32 = jnp.float32
    proj = h @ w_in
    q, k, v, o, i_pre, f_pre, z, xbc, dt, glu = split_cols(proj, IN_SIZES)

    qk = jax.nn.silu(causal_dwconv(jnp.concatenate([q, k], axis=-1), mlstm_conv_w, mlstm_conv_b))
    q, k = qk[..., :M_WIDTH], qk[..., M_WIDTH:]
    heads = lambda a: a.reshape(bsz, s, M_HEADS, M_HEAD_DIM)
    hm = mlstm_chunkwise(heads(q), heads(k), heads(v),
                         i_pre + mlstm_gate_b[:M_HEADS], f_pre + mlstm_gate_b[M_HEADS:])
    hm = jax.nn.sigmoid(o) * hm.reshape(bsz, s, M_WIDTH).astype(h.dtype)
    hm = group_rmsnorm(hm, mlstm_norm, M_HEADS)

    xbc = jax.nn.silu(causal_dwconv(xbc, ssd_conv_w, ssd_conv_b))
    gn = S_GROUPS * S_STATE
    xs, Bm, Cm = xbc[..., :S_WIDTH], xbc[..., S_WIDTH:S_WIDTH + gn], xbc[..., S_WIDTH + gn:]
    dtp = jax.nn.softplus((dt + ssd_dt_bias).astype(f32))
    A = -jnp.exp(ssd_a_log.astype(f32))
    xs_h = xs.reshape(bsz, s, S_HEADS, S_HEAD_DIM)
    y = ssd_chunkwise(xs_h, dtp, A,
                      Bm.reshape(bsz, s, S_GROUPS, S_STATE), Cm.reshape(bsz, s, S_GROUPS, S_STATE))
    y = y + ssd_d.astype(f32)[:, None] * xs_h.astype(f32)
    y = y.reshape(bsz, s, S_WIDTH).astype(h.dtype) * jax.nn.silu(z)
    y = group_rmsnorm(y, ssd_norm, S_GROUPS)

    u = glu[..., :C_WIDTH] * jax.nn.sigmoid(glu[..., C_WIDTH:])
    u = causal_dwconv(u, cm_conv_w, cm_conv_b)
    u = jax.nn.silu(layernorm(u, cm_ln_g, cm_ln_b))

    return jnp.concatenate([hm, y, u], axis=-1) @ w_out


def setup_inputs(seed: int = 0) -> dict:
    key = jax.random.key(seed)
    ks = jax.random.split(key, 27)
    f32 = jnp.float32
    nrm = lambda kk, shape, scale: jax.random.normal(kk, shape, f32) * scale
    gain = lambda kk, n: 1.0 + nrm(kk, (DEPTH, n), 0.02)

    x = nrm(ks[0], (BATCH, SEQ, D_MODEL), 1.0)

    gate_noise = nrm(ks[10], (DEPTH, 2 * M_HEADS), 0.1)
    gate_base = jnp.concatenate([jnp.zeros((M_HEADS,), f32), jnp.linspace(3.0, 6.0, M_HEADS, dtype=f32)])
    dt0 = jnp.exp(jax.random.uniform(ks[14], (DEPTH, S_HEADS), f32, math.log(1e-3), math.log(1e-1)))
    dt_bias = dt0 + jnp.log(-jnp.expm1(-dt0))
    a_log = jnp.log(jax.random.uniform(ks[15], (DEPTH, S_HEADS), f32, 1.0, 16.0))

    return {
        "x": x,
        "ffn1_norm": gain(ks[1], D_MODEL),
        "ffn1_w_gate": nrm(ks[2], (DEPTH, D_MODEL, D_FF), D_MODEL ** -0.5),
        "ffn1_w_up": nrm(ks[3], (DEPTH, D_MODEL, D_FF), D_MODEL ** -0.5),
        "ffn1_w_down": nrm(ks[4], (DEPTH, D_FF, D_MODEL), D_FF ** -0.5),
        "mix_norm": gain(ks[5], D_MODEL),
        "w_in": nrm(ks[6], (DEPTH, D_MODEL, D_IN_PROJ), D_MODEL ** -0.5),
        "w_out": nrm(ks[7], (DEPTH, D_MIX, D_MODEL), D_MIX ** -0.5),
        "mlstm_conv_w": nrm(ks[8], (DEPTH, M_CONV, 2 * M_WIDTH), M_CONV ** -0.5),
        "mlstm_conv_b": nrm(ks[9], (DEPTH, 2 * M_WIDTH), 0.02),
        "mlstm_gate_b": gate_base + gate_noise,
        "mlstm_norm": gain(ks[11], M_WIDTH),
        "ssd_conv_w": nrm(ks[12], (DEPTH, S_CONV, S_WIDTH + 2 * S_GROUPS * S_STATE), S_CONV ** -0.5),
        "ssd_conv_b": nrm(ks[13], (DEPTH, S_WIDTH + 2 * S_GROUPS * S_STATE), 0.02),
        "ssd_dt_bias": dt_bias,
        "ssd_a_log": a_log,
        "ssd_d": 1.0 + nrm(ks[16], (DEPTH, S_HEADS), 0.1),
        "ssd_norm": gain(ks[17], S_WIDTH),
        "cm_conv_w": nrm(ks[18], (DEPTH, C_KERNEL, C_WIDTH), C_KERNEL ** -0.5),
        "cm_conv_b": nrm(ks[19], (DEPTH, C_WIDTH), 0.02),
        "cm_ln_g": gain(ks[20], C_WIDTH),
        "cm_ln_b": nrm(ks[21], (DEPTH, C_WIDTH), 0.02),
        "ffn2_norm": gain(ks[22], D_MODEL),
        "ffn2_w_gate": nrm(ks[23], (DEPTH, D_MODEL, D_FF), D_MODEL ** -0.5),
        "ffn2_w_up": nrm(ks[24], (DEPTH, D_MODEL, D_FF), D_MODEL ** -0.5),
        "ffn2_w_down": nrm(ks[25], (DEPTH, D_FF, D_MODEL), D_FF ** -0.5),
        "final_norm": 1.0 + nrm(ks[26], (D_MODEL,), 0.02),
    }


def reference(x, ffn1_norm, ffn1_w_gate, ffn1_w_up, ffn1_w_down, mix_norm, w_in, w_out,
              mlstm_conv_w, mlstm_conv_b, mlstm_gate_b, mlstm_norm,
              ssd_conv_w, ssd_conv_b, ssd_dt_bias, ssd_a_log, ssd_d, ssd_norm,
              cm_conv_w, cm_conv_b, cm_ln_g, cm_ln_b,
              ffn2_norm, ffn2_w_gate, ffn2_w_up, ffn2_w_down, final_norm):
    for l in range(DEPTH):
        x = x + 0.5 * swiglu(rmsnorm(x, ffn1_norm[l]), ffn1_w_gate[l], ffn1_w_up[l], ffn1_w_down[l])
        x = x + hybrid_mixer(rmsnorm(x, mix_norm[l]), w_in[l], w_out[l],
                             mlstm_conv_w[l], mlstm_conv_b[l], mlstm_gate_b[l], mlstm_norm[l],
                             ssd_conv_w[l], ssd_conv_b[l], ssd_dt_bias[l], ssd_a_log[l], ssd_d[l], ssd_norm[l],
                             cm_conv_w[l], cm_conv_b[l], cm_ln_g[l], cm_ln_b[l])
        x = x + 0.5 * swiglu(rmsnorm(x, ffn2_norm[l]), ffn2_w_gate[l], ffn2_w_up[l], ffn2_w_down[l])
    return rmsnorm(x, final_norm)
```

```python
import functools

import jax
import jax.numpy as jnp
from jax import lax
from jax.experimental import pallas as pl
from jax.experimental.pallas import tpu as pltpu

F32 = jnp.float32
BF16 = jnp.bfloat16

RMS_EPS = 1e-6
LN_EPS = 1e-5
M_INIT_LOG = -1e30

CHUNK = 128
LANES = 128
M_HEADS = 4
M_HEAD_DIM = 128
M_WIDTH = M_HEADS * M_HEAD_DIM
M_CONV = 4
S_HEADS = 16
S_HEAD_DIM = 64
S_GROUPS = 4
S_STATE = 128
S_WIDTH = S_HEADS * S_HEAD_DIM
S_GROUP_WIDTH = S_WIDTH // S_GROUPS
S_HEADS_PER_GROUP = S_HEADS // S_GROUPS
S_CONV = 4
C_WIDTH = 512
C_KERNEL = 31
CONV_CARRY = 8
C_CARRY = 32

P_QKVO = 0
P_XBC = 2048
P_Z = 4096
P_GLU = 5120
P_IF = 6144
P_DT = 6272
P_COLS = 6400

TM_DENSE = 512
TM_INPROJ = 1024
TN_INPROJ = 1280
TM_MIX = 512
VMEM_LIMIT = 56 * 1024 * 1024


def _sigmoid(x):
    return 1.0 / (1.0 + jnp.exp(-x))


def _silu(x):
    return x * _sigmoid(x)


def _softplus(x):
    return jnp.maximum(x, 0.0) + jnp.log1p(jnp.exp(-jnp.abs(x)))


def _log_sigmoid(x):
    return jnp.minimum(x, 0.0) - jnp.log1p(jnp.exp(-jnp.abs(x)))


def _rms(x, g):
    return x * lax.rsqrt(jnp.mean(x * x, axis=-1, keepdims=True) + RMS_EPS) * g


def _dot(a, b):
    return jnp.dot(a, b, preferred_element_type=F32)


def _dot_nt(a, b):
    return lax.dot_general(a, b, (((1,), (1,)), ((), ())), preferred_element_type=F32)


def _dot_tn(a, b):
    return lax.dot_general(a, b, (((0,), (0,)), ((), ())), preferred_element_type=F32)


def _split3(a):
    hi = a.astype(BF16)
    r = a - hi.astype(F32)
    mid = r.astype(BF16)
    lo = (r - mid.astype(F32)).astype(BF16)
    return hi, mid, lo


def _dot_split_rhs(a_bf16, b_f32):
    hi, mid, lo = _split3(b_f32)
    return _dot(a_bf16, hi) + _dot(a_bf16, mid) + _dot(a_bf16, lo)


def _dot_split_lhs(a_f32, b_bf16):
    hi, mid, lo = _split3(a_f32)
    return _dot(hi, b_bf16) + _dot(mid, b_bf16) + _dot(lo, b_bf16)


def _expand(v, e):
    hi = v.astype(BF16)
    lo = (v - hi.astype(F32)).astype(BF16)
    return _dot(hi, e) + _dot(lo, e)


def _causal_mask():
    t = lax.broadcasted_iota(jnp.int32, (CHUNK, CHUNK), 0)
    s = lax.broadcasted_iota(jnp.int32, (CHUNK, CHUNK), 1)
    return s <= t


def _ffn_kernel(x_ref, g_ref, wg_ref, wu_ref, wd_ref, fg_ref, o_ref, *, final):
    x = x_ref[...]
    h = _rms(x, g_ref[...]).astype(BF16)
    a = _dot(h, wg_ref[...])
    b = _dot(h, wu_ref[...])
    t = (_silu(a) * b).astype(BF16)
    y = x + 0.5 * _dot(t, wd_ref[...])
    if final:
        y = _rms(y, fg_ref[...])
    o_ref[...] = y


def _ffn(x, g, wg, wu, wd, fg, final):
    t, d = x.shape
    dff = wg.shape[1]
    tm = TM_DENSE
    resident = functools.partial(pl.BlockSpec, pipeline_mode=pl.Buffered(1))
    return pl.pallas_call(
        functools.partial(_ffn_kernel, final=final),
        name="ffn_final" if final else "ffn",
        grid=(t // tm,),
        in_specs=[
            pl.BlockSpec((tm, d), lambda i: (i, 0)),
            pl.BlockSpec((1, d), lambda i: (0, 0)),
            resident((d, dff), lambda i: (0, 0)),
            resident((d, dff), lambda i: (0, 0)),
            resident((dff, d), lambda i: (0, 0)),
            pl.BlockSpec((1, d), lambda i: (0, 0)),
        ],
        out_specs=pl.BlockSpec((tm, d), lambda i: (i, 0)),
        out_shape=jax.ShapeDtypeStruct((t, d), F32),
        compiler_params=pltpu.CompilerParams(
            dimension_semantics=("arbitrary",), vmem_limit_bytes=VMEM_LIMIT),
    )(x, g, wg, wu, wd, fg)


def _inproj_kernel(x_ref, g_ref, w_ref, o_ref, h_ref):
    @pl.when(pl.program_id(1) == 0)
    def _():
        h_ref[...] = _rms(x_ref[...], g_ref[...]).astype(BF16)

    o_ref[...] = _dot(h_ref[...], w_ref[...])


def _inproj(x, g, w):
    t, d = x.shape
    n = w.shape[1]
    tm, tn = TM_INPROJ, TN_INPROJ
    return pl.pallas_call(
        _inproj_kernel,
        name="inproj",
        grid=(t // tm, n // tn),
        in_specs=[
            pl.BlockSpec((tm, d), lambda i, j: (i, 0)),
            pl.BlockSpec((1, d), lambda i, j: (0, 0)),
            pl.BlockSpec((d, tn), lambda i, j: (0, j)),
        ],
        out_specs=pl.BlockSpec((tm, tn), lambda i, j: (i, j)),
        out_shape=jax.ShapeDtypeStruct((t, n), F32),
        scratch_shapes=[pltpu.VMEM((tm, d), BF16)],
        compiler_params=pltpu.CompilerParams(
            dimension_semantics=("arbitrary", "arbitrary"), vmem_limit_bytes=VMEM_LIMIT),
    )(x, g, w)


def _outproj_kernel(x_ref, hm_ref, y_ref, u_ref, w_ref, o_ref):
    acc = _dot(hm_ref[...], w_ref[0:M_WIDTH, :])
    acc = acc + _dot(y_ref[...], w_ref[M_WIDTH:M_WIDTH + S_WIDTH, :])
    acc = acc + _dot(u_ref[...], w_ref[M_WIDTH + S_WIDTH:, :])
    o_ref[...] = x_ref[...] + acc


def _outproj(x, hm, y, u, w):
    t, d = x.shape
    tm = TM_DENSE
    return pl.pallas_call(
        _outproj_kernel,
        name="outproj",
        grid=(t // tm,),
        in_specs=[
            pl.BlockSpec((tm, d), lambda i: (i, 0)),
            pl.BlockSpec((tm, M_WIDTH), lambda i: (i, 0)),
            pl.BlockSpec((tm, S_WIDTH), lambda i: (i, 0)),
            pl.BlockSpec((tm, C_WIDTH), lambda i: (i, 0)),
            pl.BlockSpec(w.shape, lambda i: (0, 0)),
        ],
        out_specs=pl.BlockSpec((tm, d), lambda i: (i, 0)),
        out_shape=jax.ShapeDtypeStruct((t, d), F32),
        compiler_params=pltpu.CompilerParams(
            dimension_semantics=("arbitrary",), vmem_limit_bytes=VMEM_LIMIT),
    )(x, hm, y, u, w)


def _mlstm_kernel(qkvo_ref, if_ref, cw_ref, cb_ref, gb_ref, nrm_ref, tri_ref, trit_ref,
                  o_ref, ext_ref, q_ref, k_ref, g_ref, c_ref, m_ref):
    tm = qkvo_ref.shape[0]
    qk_w = 2 * M_WIDTH

    @pl.when(pl.program_id(1) == 0)
    def _():
        ext_ref[0:CONV_CARRY, :] = jnp.zeros((CONV_CARRY, qk_w), F32)
        c_ref[...] = jnp.zeros(c_ref.shape, F32)
        m_ref[...] = jnp.full(m_ref.shape, M_INIT_LOG, F32)

    ext_ref[CONV_CARRY:, :] = qkvo_ref[:, 0:qk_w]
    for c in range(tm // CHUNK):
        r0 = c * CHUNK
        acc = cb_ref[...]
        for j in range(M_CONV):
            off = CONV_CARRY - (M_CONV - 1) + j + r0
            acc = acc + cw_ref[j:j + 1, :] * ext_ref[off:off + CHUNK, :]
        qk = _silu(acc)
        q_ref[r0:r0 + CHUNK, :] = qk[:, :M_WIDTH].astype(BF16)
        k_ref[r0:r0 + CHUNK, :] = qk[:, M_WIDTH:] * (M_HEAD_DIM ** -0.5)
    ext_ref[0:CONV_CARRY, :] = qkvo_ref[tm - CONV_CARRY:tm, 0:qk_w]

    gg = if_ref[...] + gb_ref[...]
    lane = lax.broadcasted_iota(jnp.int32, gg.shape, 1)
    g_ref[...] = jnp.where((lane >= M_HEADS) & (lane < 2 * M_HEADS), _log_sigmoid(gg), gg)

    causal = _causal_mask()
    ones = jnp.ones((CHUNK, M_HEAD_DIM), BF16)

    def chunk_body(c, carry):
        r0 = pl.multiple_of(c * CHUNK, CHUNK)
        rows = pl.ds(r0, CHUNK)
        gc = g_ref[rows, :]
        gt = gc.T
        b_cols = _dot_split_rhs(tri_ref[...], gc)
        b_rows = _dot_split_lhs(gt[0:8, :], trit_ref[...])
        for h in range(M_HEADS):
            cols = slice(h * M_HEAD_DIM, (h + 1) * M_HEAD_DIM)
            q_h = q_ref[rows, cols]
            k_h = k_ref[rows, cols]
            v_h = qkvo_ref[rows, 2 * M_WIDTH + h * M_HEAD_DIM:2 * M_WIDTH + (h + 1) * M_HEAD_DIM]
            o_h = qkvo_ref[rows, 3 * M_WIDTH + h * M_HEAD_DIM:3 * M_WIDTH + (h + 1) * M_HEAD_DIM]
            b_col = b_cols[:, M_HEADS + h:M_HEADS + h + 1]
            i_col = gc[:, h:h + 1]
            b_row = b_rows[M_HEADS + h:M_HEADS + h + 1, :]
            i_row = gt[h:h + 1, :]
            m = m_ref[h, 0:1, 0:1]
            g = b_col[CHUNK - 1:CHUNK, :]

            dmat = jnp.where(causal, (b_col - b_row) + i_row, -jnp.inf)
            inter = b_col + m
            m_t = jnp.maximum(inter, jnp.max(dmat, axis=1, keepdims=True))
            w = jnp.exp(dmat - m_t)
            sqk = (_dot_nt(q_h, k_h.astype(BF16)) * w).astype(BF16)
            a = jnp.exp(inter - m_t)
            v_aug = jnp.concatenate([v_h.astype(BF16), ones], axis=1)
            c_aug = c_ref[h]
            res = _dot(sqk, v_aug) + a * _dot(q_h, c_aug.astype(BF16))
            num = res[:, :M_HEAD_DIM]
            den = res[:, M_HEAD_DIM:M_HEAD_DIM + 1]
            hh = num / jnp.maximum(jnp.abs(den), jnp.exp(-m_t))

            u = (g - b_col) + i_col
            m_new = jnp.maximum(g + m, jnp.max(u, axis=0, keepdims=True))
            ws = jnp.exp(u - m_new)
            decay = jnp.exp(g + m - m_new)
            kw = (k_h * ws).astype(BF16)
            c_ref[h] = decay * c_aug + _dot_tn(kw, v_aug)
            m_ref[h] = jnp.broadcast_to(m_new, m_ref.shape[1:])

            out = _sigmoid(o_h) * hh
            o_ref[rows, cols] = _rms(out, nrm_ref[:, cols]).astype(o_ref.dtype)
        return carry

    lax.fori_loop(0, tm // CHUNK, chunk_body, 0)


def _mlstm(p, cw, cb, gb, nrm, tri, trit, batch, seq):
    tm = TM_MIX
    nt = seq // tm
    const = lambda shape: pl.BlockSpec(shape, lambda b, i: (0,) * len(shape))
    return pl.pallas_call(
        _mlstm_kernel,
        name="mlstm",
        grid=(batch, nt),
        in_specs=[
            pl.BlockSpec((tm, 4 * M_WIDTH), lambda b, i: (b * nt + i, P_QKVO // (4 * M_WIDTH))),
            pl.BlockSpec((tm, LANES), lambda b, i: (b * nt + i, P_IF // LANES)),
            const(cw.shape), const(cb.shape), const(gb.shape), const(nrm.shape),
            const(tri.shape), const(trit.shape),
        ],
        out_specs=pl.BlockSpec((tm, M_WIDTH), lambda b, i: (b * nt + i, 0)),
        out_shape=jax.ShapeDtypeStruct((batch * seq, M_WIDTH), BF16),
        scratch_shapes=[
            pltpu.VMEM((tm + CONV_CARRY, 2 * M_WIDTH), F32),
            pltpu.VMEM((tm, M_WIDTH), BF16),
            pltpu.VMEM((tm, M_WIDTH), F32),
            pltpu.VMEM((tm, LANES), F32),
            pltpu.VMEM((M_HEADS, M_HEAD_DIM, 2 * M_HEAD_DIM), F32),
            pltpu.VMEM((M_HEADS, 8, LANES), F32),
        ],
        compiler_params=pltpu.CompilerParams(
            dimension_semantics=("arbitrary", "arbitrary"), vmem_limit_bytes=VMEM_LIMIT),
    )(p, p, cw, cb, gb, nrm, tri, trit)


def _ssd_kernel(xbc_ref, z_ref, dt_ref, cw_ref, cb_ref, dtb_ref, alog_ref, dskip_ref, nrm_ref,
                tri_ref, trit_ref, e_ref, o_ref, ext_ref, xc_ref, dts_ref, h_ref):
    tm = xbc_ref.shape[0]
    width = xbc_ref.shape[1]

    @pl.when(pl.program_id(1) == 0)
    def _():
        ext_ref[0:CONV_CARRY, :] = jnp.zeros((CONV_CARRY, width), F32)
        h_ref[...] = jnp.zeros(h_ref.shape, F32)

    ext_ref[CONV_CARRY:, :] = xbc_ref[...]
    for c in range(tm // CHUNK):
        r0 = c * CHUNK
        acc = cb_ref[...]
        for j in range(S_CONV):
            off = CONV_CARRY - (S_CONV - 1) + j + r0
            acc = acc + cw_ref[j:j + 1, :] * ext_ref[off:off + CHUNK, :]
        xc_ref[r0:r0 + CHUNK, :] = _silu(acc)
    ext_ref[0:CONV_CARRY, :] = xbc_ref[tm - CONV_CARRY:tm, :]

    dtv = _softplus(dt_ref[...] + dtb_ref[...])
    lane = lax.broadcasted_iota(jnp.int32, dtv.shape, 1)
    dts_ref[...] = jnp.where(lane < S_HEADS, dtv, 0.0)

    causal = _causal_mask()
    a_neg = -jnp.exp(alog_ref[...])
    seg = lax.broadcasted_iota(jnp.int32, (CHUNK, S_GROUP_WIDTH), 1) // S_HEAD_DIM

    def chunk_body(c, carry):
        r0 = pl.multiple_of(c * CHUNK, CHUNK)
        rows = pl.ds(r0, CHUNK)
        dtc = dts_ref[rows, :]
        adt = dtc * a_neg
        adt_t = adt.T
        dt_t = dtc.T
        cs_col = _dot_split_rhs(tri_ref[...], adt)
        cs_row = _dot_split_lhs(adt_t[0:S_HEADS, :], trit_ref[...])
        cs_last = cs_col[CHUNK - 1:CHUNK, :]
        ex = _expand(jnp.exp(cs_col), e_ref[...])
        dx = _expand(jnp.exp(cs_last - cs_col) * dtc, e_ref[...])
        elx = _expand(jnp.broadcast_to(jnp.exp(cs_last), (8, LANES)), e_ref[...])[0:1, :]

        for g in range(S_GROUPS):
            gcols = slice(g * S_GROUP_WIDTH, (g + 1) * S_GROUP_WIDTH)
            xs = xc_ref[rows, gcols]
            xs_b = xs.astype(BF16)
            b_g = xc_ref[rows, S_WIDTH + g * S_STATE:S_WIDTH + (g + 1) * S_STATE].astype(BF16)
            c_off = S_WIDTH + S_GROUPS * S_STATE
            c_g = xc_ref[rows, c_off + g * S_STATE:c_off + (g + 1) * S_STATE].astype(BF16)
            cbm = _dot_nt(c_g, b_g)
            h_g = h_ref[:, gcols]
            y_off = _dot(c_g, h_g.astype(BF16)) * ex[:, gcols]
            ms, xst = [], []
            for r in range(S_HEADS_PER_GROUP):
                j = g * S_HEADS_PER_GROUP + r
                lmat = jnp.exp(jnp.where(causal, cs_col[:, j:j + 1] - cs_row[j:j + 1, :], -jnp.inf))
                ms.append((cbm * lmat * dt_t[j:j + 1, :]).astype(BF16))
                xst.append(jnp.where(seg == r, xs_b, jnp.zeros_like(xs_b)))
            y_diag = _dot(jnp.concatenate(ms, axis=1), jnp.concatenate(xst, axis=0))
            xd = (xs * dx[:, gcols]).astype(BF16)
            h_ref[:, gcols] = elx[:, gcols] * h_g + _dot_tn(b_g, xd)

            y = y_diag + y_off + dskip_ref[:, gcols] * xs
            y = y * _silu(z_ref[rows, gcols])
            o_ref[rows, gcols] = _rms(y, nrm_ref[:, gcols]).astype(o_ref.dtype)
        return carry

    lax.fori_loop(0, tm // CHUNK, chunk_body, 0)


def _ssd(p, cw, cb, dtb, alog, dskip, nrm, tri, trit, e, batch, seq):
    tm = TM_MIX
    nt = seq // tm
    xbc_w = S_WIDTH + 2 * S_GROUPS * S_STATE
    const = lambda shape: pl.BlockSpec(shape, lambda b, i: (0,) * len(shape))
    return pl.pallas_call(
        _ssd_kernel,
        name="ssd",
        grid=(batch, nt),
        in_specs=[
            pl.BlockSpec((tm, xbc_w), lambda b, i: (b * nt + i, P_XBC // xbc_w)),
            pl.BlockSpec((tm, S_WIDTH), lambda b, i: (b * nt + i, P_Z // S_WIDTH)),
            pl.BlockSpec((tm, LANES), lambda b, i: (b * nt + i, P_DT // LANES)),
            const(cw.shape), const(cb.shape), const(dtb.shape), const(alog.shape),
            const(dskip.shape), const(nrm.shape), const(tri.shape), const(trit.shape),
            const(e.shape),
        ],
        out_specs=pl.BlockSpec((tm, S_WIDTH), lambda b, i: (b * nt + i, 0)),
        out_shape=jax.ShapeDtypeStruct((batch * seq, S_WIDTH), BF16),
        scratch_shapes=[
            pltpu.VMEM((tm + CONV_CARRY, xbc_w), F32),
            pltpu.VMEM((tm, xbc_w), F32),
            pltpu.VMEM((tm, LANES), F32),
            pltpu.VMEM((S_STATE, S_WIDTH), F32),
        ],
        compiler_params=pltpu.CompilerParams(
            dimension_semantics=("arbitrary", "arbitrary"), vmem_limit_bytes=VMEM_LIMIT),
    )(p, p, p, cw, cb, dtb, alog, dskip, nrm, tri, trit, e)


def _convmod_kernel(glu_ref, cw_ref, cb_ref, lg_ref, lb_ref, o_ref, ext_ref):
    tm = glu_ref.shape[0]

    @pl.when(pl.program_id(1) == 0)
    def _():
        ext_ref[0:C_CARRY, :] = jnp.zeros((C_CARRY, C_WIDTH), F32)

    ext_ref[C_CARRY:, :] = glu_ref[:, 0:C_WIDTH] * _sigmoid(glu_ref[:, C_WIDTH:])
    for c in range(tm // CHUNK):
        r0 = c * CHUNK
        acc = cb_ref[...]
        for j in range(C_KERNEL):
            off = C_CARRY - (C_KERNEL - 1) + j + r0
            acc = acc + cw_ref[j:j + 1, :] * ext_ref[off:off + CHUNK, :]
        mu = jnp.mean(acc, axis=-1, keepdims=True)
        xc = acc - mu
        y = xc * lax.rsqrt(jnp.mean(xc * xc, axis=-1, keepdims=True) + LN_EPS)
        y = y * lg_ref[...] + lb_ref[...]
        o_ref[r0:r0 + CHUNK, :] = _silu(y).astype(o_ref.dtype)
    ext_ref[0:C_CARRY, :] = ext_ref[tm:tm + C_CARRY, :]


def _convmod(p, cw, cb, lg, lb, batch, seq):
    tm = TM_MIX
    nt = seq // tm
    const = lambda shape: pl.BlockSpec(shape, lambda b, i: (0,) * len(shape))
    return pl.pallas_call(
        _convmod_kernel,
        name="convmod",
        grid=(batch, nt),
        in_specs=[
            pl.BlockSpec((tm, 2 * C_WIDTH), lambda b, i: (b * nt + i, P_GLU // (2 * C_WIDTH))),
            const(cw.shape), const(cb.shape), const(lg.shape), const(lb.shape),
        ],
        out_specs=pl.BlockSpec((tm, C_WIDTH), lambda b, i: (b * nt + i, 0)),
        out_shape=jax.ShapeDtypeStruct((batch * seq, C_WIDTH), BF16),
        scratch_shapes=[pltpu.VMEM((tm + C_CARRY, C_WIDTH), F32)],
        compiler_params=pltpu.CompilerParams(
            dimension_semantics=("arbitrary", "arbitrary"), vmem_limit_bytes=VMEM_LIMIT),
    )(p, cw, cb, lg, lb)


def _pad_lanes(a, width=LANES):
    return jnp.pad(a, ((0, 0), (0, width - a.shape[1])))


def _rearrange_w_in(w):
    o_if = 4 * M_WIDTH
    o_z = o_if + 2 * M_HEADS
    o_xbc = o_z + S_WIDTH
    o_dt = o_xbc + S_WIDTH + 2 * S_GROUPS * S_STATE
    o_glu = o_dt + S_HEADS
    return jnp.concatenate([
        w[:, 0:o_if], w[:, o_xbc:o_dt], w[:, o_z:o_xbc], w[:, o_glu:],
        _pad_lanes(w[:, o_if:o_z]), _pad_lanes(w[:, o_dt:o_glu])], axis=1).astype(BF16)


def kernel(x, ffn1_norm, ffn1_w_gate, ffn1_w_up, ffn1_w_down, mix_norm, w_in, w_out,
           mlstm_conv_w, mlstm_conv_b, mlstm_gate_b, mlstm_norm,
           ssd_conv_w, ssd_conv_b, ssd_dt_bias, ssd_a_log, ssd_d, ssd_norm,
           cm_conv_w, cm_conv_b, cm_ln_g, cm_ln_b,
           ffn2_norm, ffn2_w_gate, ffn2_w_up, ffn2_w_down, final_norm):
    batch, seq, d = x.shape
    depth = w_in.shape[0]
    xf = x.reshape(batch * seq, d)

    idx = jnp.arange(CHUNK)
    tri = (idx[None, :] <= idx[:, None]).astype(BF16)
    trit = tri.T
    head_of_col = jnp.arange(S_WIDTH) // S_HEAD_DIM
    expand = (jnp.arange(LANES)[:, None] == head_of_col[None, :]).astype(BF16)
    row = lambda a: a.reshape(1, -1).astype(F32)
    fnorm = row(final_norm)

    for l in range(depth):
        xf = _ffn(xf, row(ffn1_norm[l]), ffn1_w_gate[l].astype(BF16), ffn1_w_up[l].astype(BF16),
                  ffn1_w_down[l].astype(BF16), fnorm, final=False)
        p = _inproj(xf, row(mix_norm[l]), _rearrange_w_in(w_in[l]))
        hm = _mlstm(p, mlstm_conv_w[l], row(mlstm_conv_b[l]), _pad_lanes(row(mlstm_gate_b[l])),
                    row(mlstm_norm[l]), tri, trit, batch, seq)
        y = _ssd(p, ssd_conv_w[l], row(ssd_conv_b[l]), _pad_lanes(row(ssd_dt_bias[l])),
                 _pad_lanes(row(ssd_a_log[l])), row(jnp.repeat(ssd_d[l], S_HEAD_DIM)),
                 row(ssd_norm[l]), tri, trit, expand, batch, seq)
        u = _convmod(p, cm_conv_w[l], row(cm_conv_b[l]), row(cm_ln_g[l]), row(cm_ln_b[l]),
                     batch, seq)
        xf = _outproj(xf, hm, y, u, w_out[l].astype(BF16))
        xf = _ffn(xf, row(ffn2_norm[l]), ffn2_w_gate[l].astype(BF16), ffn2_w_up[l].astype(BF16),
                  ffn2_w_down[l].astype(BF16), fnorm, final=(l == depth - 1))
    return xf.reshape(batch, seq, d)
```

```python
import functools

import jax
import jax.numpy as jnp
from jax import lax
from jax.experimental import pallas as pl
from jax.experimental.pallas import tpu as pltpu

F32 = jnp.float32
BF16 = jnp.bfloat16

RMS_EPS = 1e-6
LN_EPS = 1e-5
M_INIT_LOG = -1e30

CHUNK = 128
LANES = 128
M_HEADS = 4
M_HEAD_DIM = 128
M_WIDTH = M_HEADS * M_HEAD_DIM
M_CONV = 4
S_HEADS = 16
S_HEAD_DIM = 64
S_GROUPS = 4
S_STATE = 128
S_WIDTH = S_HEADS * S_HEAD_DIM
S_GROUP_WIDTH = S_WIDTH // S_GROUPS
S_HEADS_PER_GROUP = S_HEADS // S_GROUPS
S_CONV = 4
C_WIDTH = 512
C_KERNEL = 31
CONV_CARRY = 8
C_CARRY = 32

P_QKVO = 0
P_XBC = 2048
P_Z = 4096
P_GLU = 5120
P_IF = 6144
P_DT = 6272
P_COLS = 6400

TM_DENSE = 512
TM_INPROJ = 1024
TN_INPROJ = 1280
TM_MIX = 512
VMEM_LIMIT = 56 * 1024 * 1024


def _sigmoid(x):
    return 1.0 / (1.0 + jnp.exp(-x))


def _silu(x):
    return x * _sigmoid(x)


def _softplus(x):
    return jnp.maximum(x, 0.0) + jnp.log1p(jnp.exp(-jnp.abs(x)))


def _log_sigmoid(x):
    return jnp.minimum(x, 0.0) - jnp.log1p(jnp.exp(-jnp.abs(x)))


def _rms(x, g):
    return x * lax.rsqrt(jnp.mean(x * x, axis=-1, keepdims=True) + RMS_EPS) * g


def _dot(a, b):
    return jnp.dot(a, b, preferred_element_type=F32)


def _dot_nt(a, b):
    return lax.dot_general(a, b, (((1,), (1,)), ((), ())), preferred_element_type=F32)


def _dot_tn(a, b):
    return lax.dot_general(a, b, (((0,), (0,)), ((), ())), preferred_element_type=F32)


def _split3(a):
    hi = a.astype(BF16)
    r = a - hi.astype(F32)
    mid = r.astype(BF16)
    lo = (r - mid.astype(F32)).astype(BF16)
    return hi, mid, lo


def _dot_split_rhs(a_bf16, b_f32):
    hi, mid, lo = _split3(b_f32)
    return _dot(a_bf16, hi) + _dot(a_bf16, mid) + _dot(a_bf16, lo)


def _dot_split_lhs(a_f32, b_bf16):
    hi, mid, lo = _split3(a_f32)
    return _dot(hi, b_bf16) + _dot(mid, b_bf16) + _dot(lo, b_bf16)


def _expand(v, e):
    hi = v.astype(BF16)
    lo = (v - hi.astype(F32)).astype(BF16)
    return _dot(hi, e) + _dot(lo, e)


def _causal_mask():
    t = lax.broadcasted_iota(jnp.int32, (CHUNK, CHUNK), 0)
    s = lax.broadcasted_iota(jnp.int32, (CHUNK, CHUNK), 1)
    return s <= t


def _ffn_kernel(x_ref, g_ref, wg_ref, wu_ref, wd_ref, fg_ref, o_ref, *, final):
    x = x_ref[...]
    h = _rms(x, g_ref[...]).astype(BF16)
    a = _dot(h, wg_ref[...])
    b = _dot(h, wu_ref[...])
    t = (_silu(a) * b).astype(BF16)
    y = x + 0.5 * _dot(t, wd_ref[...])
    if final:
        y = _rms(y, fg_ref[...])
    o_ref[...] = y


def _ffn(x, g, wg, wu, wd, fg, final):
    t, d = x.shape
    dff = wg.shape[1]
    tm = TM_DENSE
    resident = functools.partial(pl.BlockSpec, pipeline_mode=pl.Buffered(1))
    return pl.pallas_call(
        functools.partial(_ffn_kernel, final=final),
        name="ffn_final" if final else "ffn",
        grid=(t // tm,),
        in_specs=[
            pl.BlockSpec((tm, d), lambda i: (i, 0)),
            pl.BlockSpec((1, d), lambda i: (0, 0)),
            resident((d, dff), lambda i: (0, 0)),
            resident((d, dff), lambda i: (0, 0)),
            resident((dff, d), lambda i: (0, 0)),
            pl.BlockSpec((1, d), lambda i: (0, 0)),
        ],
        out_specs=pl.BlockSpec((tm, d), lambda i: (i, 0)),
        out_shape=jax.ShapeDtypeStruct((t, d), F32),
        compiler_params=pltpu.CompilerParams(
            dimension_semantics=("arbitrary",), vmem_limit_bytes=VMEM_LIMIT),
    )(x, g, wg, wu, wd, fg)


def _inproj_kernel(x_ref, g_ref, w_ref, o_ref, h_ref):
    @pl.when(pl.program_id(1) == 0)
    def _():
        h_ref[...] = _rms(x_ref[...], g_ref[...]).astype(BF16)

    o_ref[...] = _dot(h_ref[...], w_ref[...])


def _inproj(x, g, w):
    t, d = x.shape
    n = w.shape[1]
    tm, tn = TM_INPROJ, TN_INPROJ
    return pl.pallas_call(
        _inproj_kernel,
        name="inproj",
        grid=(t // tm, n // tn),
        in_specs=[
            pl.BlockSpec((tm, d), lambda i, j: (i, 0)),
            pl.BlockSpec((1, d), lambda i, j: (0, 0)),
            pl.BlockSpec((d, tn), lambda i, j: (0, j)),
        ],
        out_specs=pl.BlockSpec((tm, tn), lambda i, j: (i, j)),
        out_shape=jax.ShapeDtypeStruct((t, n), F32),
        scratch_shapes=[pltpu.VMEM((tm, d), BF16)],
        compiler_params=pltpu.CompilerParams(
            dimension_semantics=("arbitrary", "arbitrary"), vmem_limit_bytes=VMEM_LIMIT),
    )(x, g, w)


def _outproj_kernel(x_ref, hm_ref, y_ref, u_ref, w_ref, o_ref):
    acc = _dot(hm_ref[...], w_ref[0:M_WIDTH, :])
    acc = acc + _dot(y_ref[...], w_ref[M_WIDTH:M_WIDTH + S_WIDTH, :])
    acc = acc + _dot(u_ref[...], w_ref[M_WIDTH + S_WIDTH:, :])
    o_ref[...] = x_ref[...] + acc


def _outproj(x, hm, y, u, w):
    t, d = x.shape
    tm = TM_DENSE
    return pl.pallas_call(
        _outproj_kernel,
        name="outproj",
        grid=(t // tm,),
        in_specs=[
            pl.BlockSpec((tm, d), lambda i: (i, 0)),
            pl.BlockSpec((tm, M_WIDTH), lambda i: (i, 0)),
            pl.BlockSpec((tm, S_WIDTH), lambda i: (i, 0)),
            pl.BlockSpec((tm, C_WIDTH), lambda i: (i, 0)),
            pl.BlockSpec(w.shape, lambda i: (0, 0)),
        ],
        out_specs=pl.BlockSpec((tm, d), lambda i: (i, 0)),
        out_shape=jax.ShapeDtypeStruct((t, d), F32),
        compiler_params=pltpu.CompilerParams(
            dimension_semantics=("arbitrary",), vmem_limit_bytes=VMEM_LIMIT),
    )(x, hm, y, u, w)


def _mlstm_kernel(qkvo_ref, if_ref, cw_ref, cb_ref, gb_ref, nrm_ref, tri_ref, trit_ref,
                  o_ref, ext_ref, q_ref, k_ref, g_ref, c_ref, m_ref):
    tm = qkvo_ref.shape[0]
    qk_w = 2 * M_WIDTH

    @pl.when(pl.program_id(1) == 0)
    def _():
        ext_ref[0:CONV_CARRY, :] = jnp.zeros((CONV_CARRY, qk_w), F32)
        c_ref[...] = jnp.zeros(c_ref.shape, F32)
        m_ref[...] = jnp.full(m_ref.shape, M_INIT_LOG, F32)

    ext_ref[CONV_CARRY:, :] = qkvo_ref[:, 0:qk_w]
    for c in range(tm // CHUNK):
        r0 = c * CHUNK
        acc = cb_ref[...]
        for j in range(M_CONV):
            off = CONV_CARRY - (M_CONV - 1) + j + r0
            acc = acc + cw_ref[j:j + 1, :] * ext_ref[off:off + CHUNK, :]
        qk = _silu(acc)
        q_ref[r0:r0 + CHUNK, :] = qk[:, :M_WIDTH].astype(BF16)
        k_ref[r0:r0 + CHUNK, :] = qk[:, M_WIDTH:] * (M_HEAD_DIM ** -0.5)
    ext_ref[0:CONV_CARRY, :] = qkvo_ref[tm - CONV_CARRY:tm, 0:qk_w]

    gg = if_ref[...] + gb_ref[...]
    lane = lax.broadcasted_iota(jnp.int32, gg.shape, 1)
    g_ref[...] = jnp.where((lane >= M_HEADS) & (lane < 2 * M_HEADS), _log_sigmoid(gg), gg)

    causal = _causal_mask()
    ones = jnp.ones((CHUNK, M_HEAD_DIM), BF16)

    def chunk_body(c, carry):
        r0 = c * CHUNK
        rows = slice(r0, r0 + CHUNK)
        gc = g_ref[rows, :]
        gt = gc.T
        b_cols = _dot_split_rhs(tri_ref[...], gc)
        b_rows = _dot_split_lhs(gt[0:8, :], trit_ref[...])
        for h in range(M_HEADS):
            cols = slice(h * M_HEAD_DIM, (h + 1) * M_HEAD_DIM)
            q_h = q_ref[rows, cols]
            k_h = k_ref[rows, cols]
            v_h = qkvo_ref[rows, 2 * M_WIDTH + h * M_HEAD_DIM:2 * M_WIDTH + (h + 1) * M_HEAD_DIM]
            o_h = qkvo_ref[rows, 3 * M_WIDTH + h * M_HEAD_DIM:3 * M_WIDTH + (h + 1) * M_HEAD_DIM]
            b_col = b_cols[:, M_HEADS + h:M_HEADS + h + 1]
            i_col = gc[:, h:h + 1]
            b_row = b_rows[M_HEADS + h:M_HEADS + h + 1, :]
            i_row = gt[h:h + 1, :]
            m = m_ref[h, 0:1, 0:1]
            g = b_col[CHUNK - 1:CHUNK, :]

            dmat = jnp.where(causal, (b_col - b_row) + i_row, -jnp.inf)
            inter = b_col + m
            m_t = jnp.maximum(inter, jnp.max(dmat, axis=1, keepdims=True))
            w = jnp.exp(dmat - m_t)
            sqk = (_dot_nt(q_h, k_h.astype(BF16)) * w).astype(BF16)
            a = jnp.exp(inter - m_t)
            v_aug = jnp.concatenate([v_h.astype(BF16), ones], axis=1)
            c_aug = c_ref[h]
            res = _dot(sqk, v_aug) + a * _dot(q_h, c_aug.astype(BF16))
            num = res[:, :M_HEAD_DIM]
            den = res[:, M_HEAD_DIM:M_HEAD_DIM + 1]
            hh = num / jnp.maximum(jnp.abs(den), jnp.exp(-m_t))

            u = (g - b_col) + i_col
            m_new = jnp.maximum(g + m, jnp.max(u, axis=0, keepdims=True))
            ws = jnp.exp(u - m_new)
            decay = jnp.exp(g + m - m_new)
            kw = (k_h * ws).astype(BF16)
            c_ref[h] = decay * c_aug + _dot_tn(kw, v_aug)
            m_ref[h] = jnp.broadcast_to(m_new, m_ref.shape[1:])

            out = _sigmoid(o_h) * hh
            o_ref[rows, cols] = _rms(out, nrm_ref[:, cols]).astype(o_ref.dtype)
        return carry

    for c in range(tm // CHUNK):
        chunk_body(c, 0)


def _mlstm(p, cw, cb, gb, nrm, tri, trit, batch, seq):
    tm = TM_MIX
    nt = seq // tm
    const = lambda shape: pl.BlockSpec(shape, lambda b, i: (0,) * len(shape))
    return pl.pallas_call(
        _mlstm_kernel,
        name="mlstm",
        grid=(batch, nt),
        in_specs=[
            pl.BlockSpec((tm, 4 * M_WIDTH), lambda b, i: (b * nt + i, P_QKVO // (4 * M_WIDTH))),
            pl.BlockSpec((tm, LANES), lambda b, i: (b * nt + i, P_IF // LANES)),
            const(cw.shape), const(cb.shape), const(gb.shape), const(nrm.shape),
            const(tri.shape), const(trit.shape),
        ],
        out_specs=pl.BlockSpec((tm, M_WIDTH), lambda b, i: (b * nt + i, 0)),
        out_shape=jax.ShapeDtypeStruct((batch * seq, M_WIDTH), BF16),
        scratch_shapes=[
            pltpu.VMEM((tm + CONV_CARRY, 2 * M_WIDTH), F32),
            pltpu.VMEM((tm, M_WIDTH), BF16),
            pltpu.VMEM((tm, M_WIDTH), F32),
            pltpu.VMEM((tm, LANES), F32),
            pltpu.VMEM((M_HEADS, M_HEAD_DIM, 2 * M_HEAD_DIM), F32),
            pltpu.VMEM((M_HEADS, 8, LANES), F32),
        ],
        compiler_params=pltpu.CompilerParams(
            dimension_semantics=("arbitrary", "arbitrary"), vmem_limit_bytes=VMEM_LIMIT),
    )(p, p, cw, cb, gb, nrm, tri, trit)


def _ssd_kernel(xbc_ref, z_ref, dt_ref, cw_ref, cb_ref, dtb_ref, alog_ref, dskip_ref, nrm_ref,
                tri_ref, trit_ref, e_ref, o_ref, ext_ref, xc_ref, dts_ref, h_ref):
    tm = xbc_ref.shape[0]
    width = xbc_ref.shape[1]

    @pl.when(pl.program_id(1) == 0)
    def _():
        ext_ref[0:CONV_CARRY, :] = jnp.zeros((CONV_CARRY, width), F32)
        h_ref[...] = jnp.zeros(h_ref.shape, F32)

    ext_ref[CONV_CARRY:, :] = xbc_ref[...]
    for c in range(tm // CHUNK):
        r0 = c * CHUNK
        acc = cb_ref[...]
        for j in range(S_CONV):
            off = CONV_CARRY - (S_CONV - 1) + j + r0
            acc = acc + cw_ref[j:j + 1, :] * ext_ref[off:off + CHUNK, :]
        xc_ref[r0:r0 + CHUNK, :] = _silu(acc)
    ext_ref[0:CONV_CARRY, :] = xbc_ref[tm - CONV_CARRY:tm, :]

    dtv = _softplus(dt_ref[...] + dtb_ref[...])
    lane = lax.broadcasted_iota(jnp.int32, dtv.shape, 1)
    dts_ref[...] = jnp.where(lane < S_HEADS, dtv, 0.0)

    causal = _causal_mask()
    a_neg = -jnp.exp(alog_ref[...])
    seg = lax.broadcasted_iota(jnp.int32, (CHUNK, S_GROUP_WIDTH), 1) // S_HEAD_DIM

    def chunk_body(c, carry):
        r0 = c * CHUNK
        rows = slice(r0, r0 + CHUNK)
        dtc = dts_ref[rows, :]
        adt = dtc * a_neg
        adt_t = adt.T
        dt_t = dtc.T
        cs_col = _dot_split_rhs(tri_ref[...], adt)
        cs_row = _dot_split_lhs(adt_t[0:S_HEADS, :], trit_ref[...])
        cs_last = cs_col[CHUNK - 1:CHUNK, :]
        ex = _expand(jnp.exp(cs_col), e_ref[...])
        dx = _expand(jnp.exp(cs_last - cs_col) * dtc, e_ref[...])
        elx = _expand(jnp.broadcast_to(jnp.exp(cs_last), (8, LANES)), e_ref[...])[0:1, :]

        for g in range(S_GROUPS):
            gcols = slice(g * S_GROUP_WIDTH, (g + 1) * S_GROUP_WIDTH)
            xs = xc_ref[rows, gcols]
            xs_b = xs.astype(BF16)
            b_g = xc_ref[rows, S_WIDTH + g * S_STATE:S_WIDTH + (g + 1) * S_STATE].astype(BF16)
            c_off = S_WIDTH + S_GROUPS * S_STATE
            c_g = xc_ref[rows, c_off + g * S_STATE:c_off + (g + 1) * S_STATE].astype(BF16)
            cbm = _dot_nt(c_g, b_g)
            h_g = h_ref[:, gcols]
            y_off = _dot(c_g, h_g.astype(BF16)) * ex[:, gcols]
            ms, xst = [], []
            for r in range(S_HEADS_PER_GROUP):
                j = g * S_HEADS_PER_GROUP + r
                lmat = jnp.exp(jnp.where(causal, cs_col[:, j:j + 1] - cs_row[j:j + 1, :], -jnp.inf))
                ms.append((cbm * lmat * dt_t[j:j + 1, :]).astype(BF16))
                xst.append(jnp.where(seg == r, xs_b, jnp.zeros_like(xs_b)))
            y_diag = _dot(jnp.concatenate(ms, axis=1), jnp.concatenate(xst, axis=0))
            xd = (xs * dx[:, gcols]).astype(BF16)
            h_ref[:, gcols] = elx[:, gcols] * h_g + _dot_tn(b_g, xd)

            y = y_diag + y_off + dskip_ref[:, gcols] * xs
            y = y * _silu(z_ref[rows, gcols])
            o_ref[rows, gcols] = _rms(y, nrm_ref[:, gcols]).astype(o_ref.dtype)
        return carry

    for c in range(tm // CHUNK):
        chunk_body(c, 0)


def _ssd(p, cw, cb, dtb, alog, dskip, nrm, tri, trit, e, batch, seq):
    tm = TM_MIX
    nt = seq // tm
    xbc_w = S_WIDTH + 2 * S_GROUPS * S_STATE
    const = lambda shape: pl.BlockSpec(shape, lambda b, i: (0,) * len(shape))
    return pl.pallas_call(
        _ssd_kernel,
        name="ssd",
        grid=(batch, nt),
        in_specs=[
            pl.BlockSpec((tm, xbc_w), lambda b, i: (b * nt + i, P_XBC // xbc_w)),
            pl.BlockSpec((tm, S_WIDTH), lambda b, i: (b * nt + i, P_Z // S_WIDTH)),
            pl.BlockSpec((tm, LANES), lambda b, i: (b * nt + i, P_DT // LANES)),
            const(cw.shape), const(cb.shape), const(dtb.shape), const(alog.shape),
            const(dskip.shape), const(nrm.shape), const(tri.shape), const(trit.shape),
            const(e.shape),
        ],
        out_specs=pl.BlockSpec((tm, S_WIDTH), lambda b, i: (b * nt + i, 0)),
        out_shape=jax.ShapeDtypeStruct((batch * seq, S_WIDTH), BF16),
        scratch_shapes=[
            pltpu.VMEM((tm + CONV_CARRY, xbc_w), F32),
            pltpu.VMEM((tm, xbc_w), F32),
            pltpu.VMEM((tm, LANES), F32),
            pltpu.VMEM((S_STATE, S_WIDTH), F32),
        ],
        compiler_params=pltpu.CompilerParams(
            dimension_semantics=("arbitrary", "arbitrary"), vmem_limit_bytes=VMEM_LIMIT),
    )(p, p, p, cw, cb, dtb, alog, dskip, nrm, tri, trit, e)


def _convmod_kernel(glu_ref, cw_ref, cb_ref, lg_ref, lb_ref, o_ref, ext_ref):
    tm = glu_ref.shape[0]

    @pl.when(pl.program_id(1) == 0)
    def _():
        ext_ref[0:C_CARRY, :] = jnp.zeros((C_CARRY, C_WIDTH), F32)

    ext_ref[C_CARRY:, :] = glu_ref[:, 0:C_WIDTH] * _sigmoid(glu_ref[:, C_WIDTH:])
    for c in range(tm // CHUNK):
        r0 = c * CHUNK
        acc = cb_ref[...]
        for j in range(C_KERNEL):
            off = C_CARRY - (C_KERNEL - 1) + j + r0
            acc = acc + cw_ref[j:j + 1, :] * ext_ref[off:off + CHUNK, :]
        mu = jnp.mean(acc, axis=-1, keepdims=True)
        xc = acc - mu
        y = xc * lax.rsqrt(jnp.mean(xc * xc, axis=-1, keepdims=True) + LN_EPS)
        y = y * lg_ref[...] + lb_ref[...]
        o_ref[r0:r0 + CHUNK, :] = _silu(y).astype(o_ref.dtype)
    ext_ref[0:C_CARRY, :] = ext_ref[tm:tm + C_CARRY, :]


def _convmod(p, cw, cb, lg, lb, batch, seq):
    tm = TM_MIX
    nt = seq // tm
    const = lambda shape: pl.BlockSpec(shape, lambda b, i: (0,) * len(shape))
    return pl.pallas_call(
        _convmod_kernel,
        name="convmod",
        grid=(batch, nt),
        in_specs=[
            pl.BlockSpec((tm, 2 * C_WIDTH), lambda b, i: (b * nt + i, P_GLU // (2 * C_WIDTH))),
            const(cw.shape), const(cb.shape), const(lg.shape), const(lb.shape),
        ],
        out_specs=pl.BlockSpec((tm, C_WIDTH), lambda b, i: (b * nt + i, 0)),
        out_shape=jax.ShapeDtypeStruct((batch * seq, C_WIDTH), BF16),
        scratch_shapes=[pltpu.VMEM((tm + C_CARRY, C_WIDTH), F32)],
        compiler_params=pltpu.CompilerParams(
            dimension_semantics=("arbitrary", "arbitrary"), vmem_limit_bytes=VMEM_LIMIT),
    )(p, cw, cb, lg, lb)


def _pad_lanes(a, width=LANES):
    return jnp.pad(a, ((0, 0), (0, width - a.shape[1])))


def _rearrange_w_in(w):
    o_if = 4 * M_WIDTH
    o_z = o_if + 2 * M_HEADS
    o_xbc = o_z + S_WIDTH
    o_dt = o_xbc + S_WIDTH + 2 * S_GROUPS * S_STATE
    o_glu = o_dt + S_HEADS
    return jnp.concatenate([
        w[:, 0:o_if], w[:, o_xbc:o_dt], w[:, o_z:o_xbc], w[:, o_glu:],
        _pad_lanes(w[:, o_if:o_z]), _pad_lanes(w[:, o_dt:o_glu])], axis=1).astype(BF16)


def kernel(x, ffn1_norm, ffn1_w_gate, ffn1_w_up, ffn1_w_down, mix_norm, w_in, w_out,
           mlstm_conv_w, mlstm_conv_b, mlstm_gate_b, mlstm_norm,
           ssd_conv_w, ssd_conv_b, ssd_dt_bias, ssd_a_log, ssd_d, ssd_norm,
           cm_conv_w, cm_conv_b, cm_ln_g, cm_ln_b,
           ffn2_norm, ffn2_w_gate, ffn2_w_up, ffn2_w_down, final_norm):
    batch, seq, d = x.shape
    depth = w_in.shape[0]
    xf = x.reshape(batch * seq, d)

    idx = jnp.arange(CHUNK)
    tri = (idx[None, :] <= idx[:, None]).astype(BF16)
    trit = tri.T
    head_of_col = jnp.arange(S_WIDTH) // S_HEAD_DIM
    expand = (jnp.arange(LANES)[:, None] == head_of_col[None, :]).astype(BF16)
    row = lambda a: a.reshape(1, -1).astype(F32)
    fnorm = row(final_norm)

    for l in range(depth):
        xf = _ffn(xf, row(ffn1_norm[l]), ffn1_w_gate[l].astype(BF16), ffn1_w_up[l].astype(BF16),
                  ffn1_w_down[l].astype(BF16), fnorm, final=False)
        p = _inproj(xf, row(mix_norm[l]), _rearrange_w_in(w_in[l]))
        hm = _mlstm(p, mlstm_conv_w[l], row(mlstm_conv_b[l]), _pad_lanes(row(mlstm_gate_b[l])),
                    row(mlstm_norm[l]), tri, trit, batch, seq)
        y = _ssd(p, ssd_conv_w[l], row(ssd_conv_b[l]), _pad_lanes(row(ssd_dt_bias[l])),
                 _pad_lanes(row(ssd_a_log[l])), row(jnp.repeat(ssd_d[l], S_HEAD_DIM)),
                 row(ssd_norm[l]), tri, trit, expand, batch, seq)
        u = _convmod(p, cm_conv_w[l], row(cm_conv_b[l]), row(cm_ln_g[l]), row(cm_ln_b[l]),
                     batch, seq)
        xf = _outproj(xf, hm, y, u, w_out[l].astype(BF16))
        xf = _ffn(xf, row(ffn2_norm[l]), ffn2_w_gate[l].astype(BF16), ffn2_w_up[l].astype(BF16),
                  ffn2_w_down[l].astype(BF16), fnorm, final=(l == depth - 1))
    return xf.reshape(batch, seq, d)
```

```python
import functools

import jax
import jax.numpy as jnp
from jax import lax
from jax.experimental import pallas as pl
from jax.experimental.pallas import tpu as pltpu

F32 = jnp.float32
BF16 = jnp.bfloat16

RMS_EPS = 1e-6
LN_EPS = 1e-5
M_INIT_LOG = -1e30

CHUNK = 128
LANES = 128
SUBLANES = 8
M_HEADS = 4
M_HEAD_DIM = 128
M_WIDTH = M_HEADS * M_HEAD_DIM
M_CONV = 4
S_HEADS = 16
S_HEAD_DIM = 64
S_GROUPS = 4
S_STATE = 128
S_WIDTH = S_HEADS * S_HEAD_DIM
S_GROUP_WIDTH = S_WIDTH // S_GROUPS
S_HEADS_PER_GROUP = S_HEADS // S_GROUPS
S_CONV = 4
XBC_WIDTH = S_WIDTH + 2 * S_GROUPS * S_STATE
C_WIDTH = 512
C_KERNEL = 31
CONV_CARRY = SUBLANES
C_CARRY = 32
D_MIX = M_WIDTH + S_WIDTH + C_WIDTH

P_QKVO = 0
P_XBC = 2048
P_Z = 4096
P_GLU = 5120
P_GATES = 6144
P_COLS = 6400

TM_DENSE = 512
TM_MIX = 512
VMEM_LIMIT = 56 * 1024 * 1024


def _sigmoid(x):
    return 1.0 / (1.0 + jnp.exp(-x))


def _silu(x):
    return x * _sigmoid(x)


def _softplus(x):
    return jnp.maximum(x, 0.0) + jnp.log1p(jnp.exp(-jnp.abs(x)))


def _log_sigmoid(x):
    return jnp.minimum(x, 0.0) - jnp.log1p(jnp.exp(-jnp.abs(x)))


def _rms(x, g):
    return x * lax.rsqrt(jnp.mean(x * x, axis=-1, keepdims=True) + RMS_EPS) * g


def _dot(a, b):
    return jnp.dot(a, b, preferred_element_type=F32)


def _dot_nt(a, b):
    return lax.dot_general(a, b, (((1,), (1,)), ((), ())), preferred_element_type=F32)


def _dot_tn(a, b):
    return lax.dot_general(a, b, (((0,), (0,)), ((), ())), preferred_element_type=F32)


def _split3(a):
    hi = a.astype(BF16)
    r = a - hi.astype(F32)
    mid = r.astype(BF16)
    lo = (r - mid.astype(F32)).astype(BF16)
    return hi, mid, lo


def _dot_split_rhs(a_bf16, b_f32):
    hi, mid, lo = _split3(b_f32)
    return _dot(a_bf16, hi) + _dot(a_bf16, mid) + _dot(a_bf16, lo)


def _dot_split_lhs(a_f32, b_bf16):
    hi, mid, lo = _split3(a_f32)
    return _dot(hi, b_bf16) + _dot(mid, b_bf16) + _dot(lo, b_bf16)


def _expand(v, e):
    hi = v.astype(BF16)
    lo = (v - hi.astype(F32)).astype(BF16)
    return _dot(hi, e) + _dot(lo, e)


def _causal_mask():
    t = lax.broadcasted_iota(jnp.int32, (CHUNK, CHUNK), 0)
    s = lax.broadcasted_iota(jnp.int32, (CHUNK, CHUNK), 1)
    return s <= t


def _causal_conv(ext_ref, r0, carry, width, w_ref, b_ref, cols):
    win = ext_ref[r0:r0 + carry + CHUNK, cols]
    acc = b_ref[:, cols]
    for k in range(SUBLANES):
        taps = [j for j in range(width) if (width - 1 - j) % SUBLANES == k]
        if not taps:
            continue
        rolled = win if k == 0 else pltpu.roll(win, k, 0)
        for j in taps:
            back = width - 1 - j
            start = carry - (back - k)
            acc = acc + w_ref[j:j + 1, cols] * rolled[start:start + CHUNK, :]
    return acc


def _ffn_kernel(x_ref, g_ref, wg_ref, wu_ref, wd_ref, fg_ref, o_ref, *, final):
    x = x_ref[...]
    h = _rms(x, g_ref[...]).astype(BF16)
    a = _dot(h, wg_ref[...])
    b = _dot(h, wu_ref[...])
    t = (_silu(a) * b).astype(BF16)
    y = x + 0.5 * _dot(t, wd_ref[...])
    if final:
        y = _rms(y, fg_ref[...])
    o_ref[...] = y


def _resident(shape):
    return pl.BlockSpec(shape, lambda *_: (0,) * len(shape), pipeline_mode=pl.Buffered(1))


def _ffn(x, g, wg, wu, wd, fg, final):
    t, d = x.shape
    tm = TM_DENSE
    return pl.pallas_call(
        functools.partial(_ffn_kernel, final=final),
        name="ffn_final" if final else "ffn",
        grid=(t // tm,),
        in_specs=[
            pl.BlockSpec((tm, d), lambda i: (i, 0)),
            _resident(g.shape), _resident(wg.shape), _resident(wu.shape), _resident(wd.shape),
            _resident(fg.shape),
        ],
        out_specs=pl.BlockSpec((tm, d), lambda i: (i, 0)),
        out_shape=jax.ShapeDtypeStruct((t, d), F32),
        compiler_params=pltpu.CompilerParams(
            dimension_semantics=("arbitrary",), vmem_limit_bytes=VMEM_LIMIT),
    )(x, g, wg, wu, wd, fg)


def _mlstm_chunk(rows, p, prm, scr, cat_ref, causal):
    extm_ref, c_ref, m_ref = scr
    r0 = rows.start
    extm_ref[CONV_CARRY + r0:CONV_CARRY + r0 + CHUNK, :] = p[:, P_QKVO:P_QKVO + 2 * M_WIDTH]
    conv = lambda cols: _silu(_causal_conv(extm_ref, r0, CONV_CARRY, M_CONV, prm["m_cw"], prm["m_cb"], cols))
    q_all = conv(slice(0, M_WIDTH)).astype(BF16)
    yield
    k_all = conv(slice(M_WIDTH, 2 * M_WIDTH)) * (M_HEAD_DIM ** -0.5)
    yield

    gg = p[:, P_GATES:P_GATES + LANES] + prm["m_gb"][...]
    lane = lax.broadcasted_iota(jnp.int32, gg.shape, 1)
    gc = jnp.where((lane >= M_HEADS) & (lane < 2 * M_HEADS), _log_sigmoid(gg), gg)
    gt = gc.T
    b_cols = _dot_split_rhs(prm["tri"][...], gc)
    b_rows = _dot_split_lhs(gt[0:SUBLANES, :], prm["trit"][...])
    ones = jnp.ones((CHUNK, M_HEAD_DIM), BF16)
    yield

    for h in range(M_HEADS):
        cols = slice(h * M_HEAD_DIM, (h + 1) * M_HEAD_DIM)
        q_h = q_all[:, cols]
        k_h = k_all[:, cols]
        v_off = P_QKVO + 2 * M_WIDTH + h * M_HEAD_DIM
        o_off = P_QKVO + 3 * M_WIDTH + h * M_HEAD_DIM
        v_h = p[:, v_off:v_off + M_HEAD_DIM]
        o_h = p[:, o_off:o_off + M_HEAD_DIM]
        b_col = b_cols[:, M_HEADS + h:M_HEADS + h + 1]
        i_col = gc[:, h:h + 1]
        b_row = b_rows[M_HEADS + h:M_HEADS + h + 1, :]
        i_row = gt[h:h + 1, :]
        m = m_ref[h, 0:1, 0:1]
        g = b_col[CHUNK - 1:CHUNK, :]

        dmat = jnp.where(causal, (b_col - b_row) + i_row, -jnp.inf)
        inter = b_col + m
        m_t = jnp.maximum(inter, jnp.max(dmat, axis=1, keepdims=True))
        w = jnp.exp(dmat - m_t)
        sqk = (_dot_nt(q_h, k_h.astype(BF16)) * w).astype(BF16)
        a = jnp.exp(inter - m_t)
        v_aug = jnp.concatenate([v_h.astype(BF16), ones], axis=1)
        c_aug = c_ref[h]
        res = _dot(sqk, v_aug) + a * _dot(q_h, c_aug.astype(BF16))
        num = res[:, :M_HEAD_DIM]
        den = res[:, M_HEAD_DIM:M_HEAD_DIM + 1]
        hh = num / jnp.maximum(jnp.abs(den), jnp.exp(-m_t))

        u = (g - b_col) + i_col
        m_new = jnp.maximum(g + m, jnp.max(u, axis=0, keepdims=True))
        ws = jnp.exp(u - m_new)
        decay = jnp.exp(g + m - m_new)
        kw = (k_h * ws).astype(BF16)
        c_ref[h] = decay * c_aug + _dot_tn(kw, v_aug)
        m_ref[h] = jnp.broadcast_to(m_new, m_ref.shape[1:])

        out = _sigmoid(o_h) * hh
        cat_ref[rows, cols] = _rms(out, prm["m_nrm"][:, cols]).astype(cat_ref.dtype)
        yield


def _ssd_chunk(rows, p, prm, scr, cat_ref, causal):
    exts_ref, h_ref = scr
    r0 = rows.start
    exts_ref[CONV_CARRY + r0:CONV_CARRY + r0 + CHUNK, :] = p[:, P_XBC:P_XBC + XBC_WIDTH]
    conv = lambda off, n: _silu(_causal_conv(exts_ref, r0, CONV_CARRY, S_CONV, prm["s_cw"], prm["s_cb"],
                                             slice(off, off + n)))

    dtv =_softplus(p[:, P_GATES + LANES:P_COLS] + prm["s_dtb"][...])
    lane = lax.broadcasted_iota(jnp.int32, dtv.shape, 1)
    dtc = jnp.where(lane < S_HEADS, dtv, 0.0)
    adt = dtc * (-jnp.exp(prm["s_alog"][...]))
    adt_t = adt.T
    dt_t = dtc.T
    e = prm["expand"][...]
    cs_col = _dot_split_rhs(prm["tri"][...], adt)
    cs_row = _dot_split_lhs(adt_t[0:S_HEADS, :], prm["trit"][...])
    cs_last = cs_col[CHUNK - 1:CHUNK, :]
    ex = _expand(jnp.exp(cs_col), e)
    dx = _expand(jnp.exp(cs_last - cs_col) * dtc, e)
    elx = _expand(jnp.broadcast_to(jnp.exp(cs_last), (SUBLANES, LANES)), e)[0:1, :]
    seg = lax.broadcasted_iota(jnp.int32, (CHUNK, S_GROUP_WIDTH), 1) // S_HEAD_DIM
    yield

    for g in range(S_GROUPS):
        gcols = slice(g * S_GROUP_WIDTH, (g + 1) * S_GROUP_WIDTH)
        xs = conv(g * S_GROUP_WIDTH, S_GROUP_WIDTH)
        xs_b = xs.astype(BF16)
        b_g = conv(S_WIDTH + g * S_STATE, S_STATE).astype(BF16)
        c_g = conv(S_WIDTH + S_GROUPS * S_STATE + g * S_STATE, S_STATE).astype(BF16)
        yield
        cbm = _dot_nt(c_g, b_g)
        h_g = h_ref[:, gcols]
        y_off = _dot(c_g, h_g.astype(BF16)) * ex[:, gcols]
        ms, xst = [], []
        for r in range(S_HEADS_PER_GROUP):
            j = g * S_HEADS_PER_GROUP + r
            lmat = jnp.exp(jnp.where(causal, cs_col[:, j:j + 1] - cs_row[j:j + 1, :], -jnp.inf))
            ms.append((cbm * lmat * dt_t[j:j + 1, :]).astype(BF16))
            xst.append(jnp.where(seg == r, xs_b, jnp.zeros_like(xs_b)))
        y_diag = _dot(jnp.concatenate(ms, axis=1), jnp.concatenate(xst, axis=0))
        xd = (xs * dx[:, gcols]).astype(BF16)
        h_ref[:, gcols] = elx[:, gcols] * h_g + _dot_tn(b_g, xd)

        y = y_diag + y_off + prm["s_dskip"][:, gcols] * xs
        y = y * _silu(p[:, P_Z + g * S_GROUP_WIDTH:P_Z + (g + 1) * S_GROUP_WIDTH])
        ocols = slice(M_WIDTH + g * S_GROUP_WIDTH, M_WIDTH + (g + 1) * S_GROUP_WIDTH)
        cat_ref[rows, ocols] = _rms(y, prm["s_nrm"][:, gcols]).astype(cat_ref.dtype)
        yield


def _convmod_chunk(rows, p, prm, extc_ref, cat_ref):
    r0 = rows.start
    gate = _sigmoid(p[:, P_GLU + C_WIDTH:P_GLU + 2 * C_WIDTH])
    extc_ref[C_CARRY + r0:C_CARRY + r0 + CHUNK, :] = p[:, P_GLU:P_GLU + C_WIDTH] * gate
    yield
    pieces = []
    for lo in range(0, C_WIDTH, LANES):
        pieces.append(_causal_conv(extc_ref, r0, C_CARRY, C_KERNEL, prm["c_cw"], prm["c_cb"],
                                   slice(lo, lo + LANES)))
        yield
    acc = jnp.concatenate(pieces, axis=1)
    mu = jnp.mean(acc, axis=-1, keepdims=True)
    xc = acc - mu
    y = xc * lax.rsqrt(jnp.mean(xc * xc, axis=-1, keepdims=True) + LN_EPS)
    y = y * prm["c_lg"][...] + prm["c_lb"][...]
    cat_ref[rows, M_WIDTH + S_WIDTH:] = _silu(y).astype(cat_ref.dtype)


_MIXER_PARAMS = ("norm", "win", "wout", "m_cw", "m_cb", "m_gb", "m_nrm", "s_cw", "s_cb", "s_dtb",
                 "s_alog", "s_dskip", "s_nrm", "c_cw", "c_cb", "c_lg", "c_lb", "tri", "trit", "expand")


P_PIECE = 640


def _project(x, prm, p):
    hc = _rms(x, prm["norm"][...]).astype(BF16)
    for lo in range(0, P_COLS, P_PIECE):
        p[:, lo:lo + P_PIECE] = _dot(hc, prm["win"][:, lo:lo + P_PIECE])
        yield


def _interleave(stages, filler, every):
    n = 0
    for gen in stages:
        for _ in gen:
            n += 1
            if n % every == 0:
                next(filler, None)
    for _ in filler:
        pass


def _mixer_kernel(x_ref, xn_ref, *refs):
    n_prm = len(_MIXER_PARAMS)
    prm = dict(zip(_MIXER_PARAMS, refs[:n_prm]))
    o_ref = refs[n_prm]
    extm_ref, exts_ref, extc_ref, c_ref, m_ref, h_ref, cat_ref, p_ref = refs[n_prm + 1:]
    tm = x_ref.shape[0]
    n_chunks = tm // CHUNK

    @pl.when((pl.program_id(0) == 0) & (pl.program_id(1) == 0))
    def _():
        for _ in _project(x_ref[0:CHUNK, :], prm, p_ref.at[0]):
            pass

    @pl.when(pl.program_id(1) == 0)
    def _():
        extm_ref[0:CONV_CARRY, :] = jnp.zeros((CONV_CARRY, extm_ref.shape[1]), F32)
        exts_ref[0:CONV_CARRY, :] = jnp.zeros((CONV_CARRY, exts_ref.shape[1]), F32)
        extc_ref[0:C_CARRY, :] = jnp.zeros((C_CARRY, extc_ref.shape[1]), F32)
        c_ref[...] = jnp.zeros(c_ref.shape, F32)
        m_ref[...] = jnp.full(m_ref.shape, M_INIT_LOG, F32)
        h_ref[...] = jnp.zeros(h_ref.shape, F32)

    causal = _causal_mask()
    for c in range(n_chunks):
        rows = slice(c * CHUNK, (c + 1) * CHUNK)
        x_next = x_ref[(c + 1) * CHUNK:(c + 2) * CHUNK, :] if c + 1 < n_chunks else xn_ref[...]
        p = p_ref.at[c % 2]
        _interleave(
            [_mlstm_chunk(rows, p, prm, (extm_ref, c_ref, m_ref), cat_ref, causal),
             _ssd_chunk(rows, p, prm, (exts_ref, h_ref), cat_ref, causal),
             _convmod_chunk(rows, p, prm, extc_ref, cat_ref)],
            _project(x_next, prm, p_ref.at[(c + 1) % 2]), every=2)
        o_ref[rows, :] = x_ref[rows, :] + _dot(cat_ref[rows, :], prm["wout"][...])

    extm_ref[0:CONV_CARRY, :] = extm_ref[tm:tm + CONV_CARRY, :]
    exts_ref[0:CONV_CARRY, :] = exts_ref[tm:tm + CONV_CARRY, :]
    extc_ref[0:C_CARRY, :] = extc_ref[tm:tm + C_CARRY, :]


def _mixer(x, prm, batch, seq):
    d = x.shape[1]
    tm = TM_MIX
    nt = seq // tm
    params = [prm[k] for k in _MIXER_PARAMS]
    n_chunks = tm // CHUNK
    assert n_chunks % 2 == 0 and seq % tm == 0
    last_chunk = batch * seq // CHUNK - 1
    next_chunk = lambda b, i: (jnp.minimum((b * nt + i + 1) * n_chunks, last_chunk), 0)
    return pl.pallas_call(
        _mixer_kernel,
        name="mixer",
        grid=(batch, nt),
        in_specs=[pl.BlockSpec((tm, d), lambda b, i: (b * nt + i, 0)),
                  pl.BlockSpec((CHUNK, d), next_chunk)]
        + [_resident(p.shape) for p in params],
        out_specs=pl.BlockSpec((tm, d), lambda b, i: (b * nt + i, 0)),
        out_shape=jax.ShapeDtypeStruct((batch * seq, d), F32),
        scratch_shapes=[
            pltpu.VMEM((tm + CONV_CARRY, 2 * M_WIDTH), F32),
            pltpu.VMEM((tm + CONV_CARRY, XBC_WIDTH), F32),
            pltpu.VMEM((tm + C_CARRY, C_WIDTH), F32),
            pltpu.VMEM((M_HEADS, M_HEAD_DIM, 2 * M_HEAD_DIM), F32),
            pltpu.VMEM((M_HEADS, SUBLANES, LANES), F32),
            pltpu.VMEM((S_STATE, S_WIDTH), F32),
            pltpu.VMEM((tm, D_MIX), BF16),
            pltpu.VMEM((2, CHUNK, P_COLS), F32),
        ],
        compiler_params=pltpu.CompilerParams(
            dimension_semantics=("arbitrary", "arbitrary"), vmem_limit_bytes=VMEM_LIMIT),
    )(x, x, *params)


def _pad_lanes(a, width=LANES):
    return jnp.pad(a, ((0, 0), (0, width - a.shape[1])))


def _rearrange_w_in(w):
    o_if = 4 * M_WIDTH
    o_z = o_if + 2 * M_HEADS
    o_xbc = o_z + S_WIDTH
    o_dt = o_xbc + XBC_WIDTH
    o_glu = o_dt + S_HEADS
    return jnp.concatenate([
        w[:, 0:o_if], w[:, o_xbc:o_dt], w[:, o_z:o_xbc], w[:, o_glu:],
        _pad_lanes(w[:, o_if:o_z]), _pad_lanes(w[:, o_dt:o_glu])], axis=1).astype(BF16)


def kernel(x, ffn1_norm, ffn1_w_gate, ffn1_w_up, ffn1_w_down, mix_norm, w_in, w_out,
           mlstm_conv_w, mlstm_conv_b, mlstm_gate_b, mlstm_norm,
           ssd_conv_w, ssd_conv_b, ssd_dt_bias, ssd_a_log, ssd_d, ssd_norm,
           cm_conv_w, cm_conv_b, cm_ln_g, cm_ln_b,
           ffn2_norm, ffn2_w_gate, ffn2_w_up, ffn2_w_down, final_norm):
    batch, seq, d = x.shape
    depth = w_in.shape[0]
    xf = x.reshape(batch * seq, d)

    idx = jnp.arange(CHUNK)
    tri = (idx[None, :] <= idx[:, None]).astype(BF16)
    head_of_col = jnp.arange(S_WIDTH) // S_HEAD_DIM
    expand = (jnp.arange(LANES)[:, None] == head_of_col[None, :]).astype(BF16)
    row = lambda a: a.reshape(1, -1).astype(F32)
    fnorm = row(final_norm)

    for l in range(depth):
        xf = _ffn(xf, row(ffn1_norm[l]), ffn1_w_gate[l].astype(BF16), ffn1_w_up[l].astype(BF16),
                  ffn1_w_down[l].astype(BF16), fnorm, final=False)
        prm = dict(
            norm=row(mix_norm[l]), win=_rearrange_w_in(w_in[l]), wout=w_out[l].astype(BF16),
            m_cw=mlstm_conv_w[l], m_cb=row(mlstm_conv_b[l]), m_gb=_pad_lanes(row(mlstm_gate_b[l])),
            m_nrm=row(mlstm_norm[l]),
            s_cw=ssd_conv_w[l], s_cb=row(ssd_conv_b[l]), s_dtb=_pad_lanes(row(ssd_dt_bias[l])),
            s_alog=_pad_lanes(row(ssd_a_log[l])), s_dskip=row(jnp.repeat(ssd_d[l], S_HEAD_DIM)),
            s_nrm=row(ssd_norm[l]),
            c_cw=cm_conv_w[l], c_cb=row(cm_conv_b[l]), c_lg=row(cm_ln_g[l]), c_lb=row(cm_ln_b[l]),
            tri=tri, trit=tri.T, expand=expand)
        xf = _mixer(xf, prm, batch, seq)
        xf = _ffn(xf, row(ffn2_norm[l]), ffn2_w_gate[l].astype(BF16), ffn2_w_up[l].astype(BF16),
                  ffn2_w_down[l].astype(BF16), fnorm, final=(l == depth - 1))
    return xf.reshape(batch, seq, d)
```

```python
import functools

import jax
import jax.numpy as jnp
from jax import lax
from jax.experimental import pallas as pl
from jax.experimental.pallas import tpu as pltpu

F32 = jnp.float32
BF16 = jnp.bfloat16

RMS_EPS = 1e-6
LN_EPS = 1e-5
M_INIT_LOG = -1e30

CHUNK = 128
LANES = 128
SUBLANES = 8
M_HEADS = 4
M_HEAD_DIM = 128
M_WIDTH = M_HEADS * M_HEAD_DIM
M_CONV = 4
S_HEADS = 16
S_HEAD_DIM = 64
S_GROUPS = 4
S_STATE = 128
S_WIDTH = S_HEADS * S_HEAD_DIM
S_GROUP_WIDTH = S_WIDTH // S_GROUPS
S_HEADS_PER_GROUP = S_HEADS // S_GROUPS
S_CONV = 4
XBC_WIDTH = S_WIDTH + 2 * S_GROUPS * S_STATE
C_WIDTH = 512
C_KERNEL = 31
CONV_CARRY = SUBLANES
C_CARRY = 32
D_MIX = M_WIDTH + S_WIDTH + C_WIDTH

P_QKVO = 0
P_XBC = 2048
P_Z = 4096
P_GLU = 5120
P_GATES = 6144
P_COLS = 6400

TM_DENSE = 512
TM_MIX = 512
VMEM_LIMIT = 56 * 1024 * 1024


def _sigmoid(x):
    return 1.0 / (1.0 + jnp.exp(-x))


def _silu(x):
    return x * _sigmoid(x)


def _softplus(x):
    return jnp.maximum(x, 0.0) + jnp.log1p(jnp.exp(-jnp.abs(x)))


def _log_sigmoid(x):
    return jnp.minimum(x, 0.0) - jnp.log1p(jnp.exp(-jnp.abs(x)))


def _rms(x, g):
    return x * lax.rsqrt(jnp.mean(x * x, axis=-1, keepdims=True) + RMS_EPS) * g


def _dot(a, b):
    return jnp.dot(a, b, preferred_element_type=F32)


def _dot_nt(a, b):
    return lax.dot_general(a, b, (((1,), (1,)), ((), ())), preferred_element_type=F32)


def _dot_tn(a, b):
    return lax.dot_general(a, b, (((0,), (0,)), ((), ())), preferred_element_type=F32)


def _split3(a):
    hi = a.astype(BF16)
    r = a - hi.astype(F32)
    mid = r.astype(BF16)
    lo = (r - mid.astype(F32)).astype(BF16)
    return hi, mid, lo


def _dot_split_rhs(a_bf16, b_f32):
    hi, mid, lo = _split3(b_f32)
    return _dot(a_bf16, hi) + _dot(a_bf16, mid) + _dot(a_bf16, lo)


def _dot_split_lhs(a_f32, b_bf16):
    hi, mid, lo = _split3(a_f32)
    return _dot(hi, b_bf16) + _dot(mid, b_bf16) + _dot(lo, b_bf16)


def _expand(v, e):
    hi = v.astype(BF16)
    lo = (v - hi.astype(F32)).astype(BF16)
    return _dot(hi, e) + _dot(lo, e)


def _causal_mask():
    t = lax.broadcasted_iota(jnp.int32, (CHUNK, CHUNK), 0)
    s = lax.broadcasted_iota(jnp.int32, (CHUNK, CHUNK), 1)
    return s <= t


def _causal_conv(ext_ref, r0, carry, width, w_ref, b_ref, cols):
    win = ext_ref[r0:r0 + carry + CHUNK, cols]
    acc = b_ref[:, cols]
    for k in range(SUBLANES):
        taps = [j for j in range(width) if (width - 1 - j) % SUBLANES == k]
        if not taps:
            continue
        rolled = win if k == 0 else pltpu.roll(win, k, 0)
        for j in taps:
            back = width - 1 - j
            start = carry - (back - k)
            acc = acc + w_ref[j:j + 1, cols] * rolled[start:start + CHUNK, :]
    return acc


def _ffn_kernel(x_ref, g_ref, wg_ref, wu_ref, wd_ref, fg_ref, o_ref, *, final):
    x = x_ref[...]
    h = _rms(x, g_ref[...]).astype(BF16)
    a = _dot(h, wg_ref[...])
    b = _dot(h, wu_ref[...])
    t = (_silu(a) * b).astype(BF16)
    y = x + 0.5 * _dot(t, wd_ref[...])
    if final:
        y = _rms(y, fg_ref[...])
    o_ref[...] = y


def _resident(shape):
    return pl.BlockSpec(shape, lambda *_: (0,) * len(shape), pipeline_mode=pl.Buffered(1))


def _ffn(x, g, wg, wu, wd, fg, final):
    t, d = x.shape
    tm = TM_DENSE
    return pl.pallas_call(
        functools.partial(_ffn_kernel, final=final),
        name="ffn_final" if final else "ffn",
        grid=(t // tm,),
        in_specs=[
            pl.BlockSpec((tm, d), lambda i: (i, 0)),
            _resident(g.shape), _resident(wg.shape), _resident(wu.shape), _resident(wd.shape),
            _resident(fg.shape),
        ],
        out_specs=pl.BlockSpec((tm, d), lambda i: (i, 0)),
        out_shape=jax.ShapeDtypeStruct((t, d), F32),
        compiler_params=pltpu.CompilerParams(
            dimension_semantics=("arbitrary",), vmem_limit_bytes=VMEM_LIMIT),
    )(x, g, wg, wu, wd, fg)


def _mlstm_chunk(rows, p, prm, scr, cat_ref, causal):
    extm_ref, c_ref, m_ref = scr
    r0 = rows.start
    extm_ref[CONV_CARRY + r0:CONV_CARRY + r0 + CHUNK, :] = p[:, P_QKVO:P_QKVO + 2 * M_WIDTH]
    conv = lambda cols: _silu(_causal_conv(extm_ref, r0, CONV_CARRY, M_CONV, prm["m_cw"], prm["m_cb"], cols))
    q_all = conv(slice(0, M_WIDTH)).astype(BF16)
    yield
    k_all = conv(slice(M_WIDTH, 2 * M_WIDTH)) * (M_HEAD_DIM ** -0.5)
    yield

    gg = p[:, P_GATES:P_GATES + LANES] + prm["m_gb"][...]
    lane = lax.broadcasted_iota(jnp.int32, gg.shape, 1)
    gc = jnp.where((lane >= M_HEADS) & (lane < 2 * M_HEADS), _log_sigmoid(gg), gg)
    gt = gc.T
    b_cols = _dot_split_rhs(prm["tri"][...], gc)
    b_rows = _dot_split_lhs(gt[0:SUBLANES, :], prm["trit"][...])
    ones = jnp.ones((CHUNK, M_HEAD_DIM), BF16)
    yield

    for h in range(M_HEADS):
        cols = slice(h * M_HEAD_DIM, (h + 1) * M_HEAD_DIM)
        q_h = q_all[:, cols]
        k_h = k_all[:, cols]
        v_off = P_QKVO + 2 * M_WIDTH + h * M_HEAD_DIM
        o_off = P_QKVO + 3 * M_WIDTH + h * M_HEAD_DIM
        v_h = p[:, v_off:v_off + M_HEAD_DIM]
        o_h = p[:, o_off:o_off + M_HEAD_DIM]
        b_col = b_cols[:, M_HEADS + h:M_HEADS + h + 1]
        i_col = gc[:, h:h + 1]
        b_row = b_rows[M_HEADS + h:M_HEADS + h + 1, :]
        i_row = gt[h:h + 1, :]
        m = m_ref[h, 0:1, 0:1]
        g = b_col[CHUNK - 1:CHUNK, :]

        dmat = jnp.where(causal, (b_col - b_row) + i_row, -jnp.inf)
        inter = b_col + m
        m_t = jnp.maximum(inter, jnp.max(dmat, axis=1, keepdims=True))
        w = jnp.exp(dmat - m_t)
        sqk = (_dot_nt(q_h, k_h.astype(BF16)) * w).astype(BF16)
        a = jnp.exp(inter - m_t)
        v_aug = jnp.concatenate([v_h.astype(BF16), ones], axis=1)
        c_aug = c_ref[h]
        res = _dot(sqk, v_aug) + a * _dot(q_h, c_aug.astype(BF16))
        num = res[:, :M_HEAD_DIM]
        den = res[:, M_HEAD_DIM:M_HEAD_DIM + 1]
        hh = num / jnp.maximum(jnp.abs(den), jnp.exp(-m_t))

        u = (g - b_col) + i_col
        m_new = jnp.maximum(g + m, jnp.max(u, axis=0, keepdims=True))
        ws = jnp.exp(u - m_new)
        decay = jnp.exp(g + m - m_new)
        kw = (k_h * ws).astype(BF16)
        c_ref[h] = decay * c_aug + _dot_tn(kw, v_aug)
        m_ref[h] = jnp.broadcast_to(m_new, m_ref.shape[1:])

        out = _sigmoid(o_h) * hh
        cat_ref[rows, cols] = _rms(out, prm["m_nrm"][:, cols]).astype(cat_ref.dtype)
        yield


def _ssd_chunk(rows, p, prm, scr, cat_ref, causal):
    exts_ref, h_ref = scr
    r0 = rows.start
    exts_ref[CONV_CARRY + r0:CONV_CARRY + r0 + CHUNK, :] = p[:, P_XBC:P_XBC + XBC_WIDTH]
    conv = lambda off, n: _silu(_causal_conv(exts_ref, r0, CONV_CARRY, S_CONV, prm["s_cw"], prm["s_cb"],
                                             slice(off, off + n)))

    dtv =_softplus(p[:, P_GATES + LANES:P_COLS] + prm["s_dtb"][...])
    lane = lax.broadcasted_iota(jnp.int32, dtv.shape, 1)
    dtc = jnp.where(lane < S_HEADS, dtv, 0.0)
    adt = dtc * (-jnp.exp(prm["s_alog"][...]))
    adt_t = adt.T
    dt_t = dtc.T
    e = prm["expand"][...]
    cs_col = _dot_split_rhs(prm["tri"][...], adt)
    cs_row = _dot_split_lhs(adt_t[0:S_HEADS, :], prm["trit"][...])
    cs_last = cs_col[CHUNK - 1:CHUNK, :]
    ex = _expand(jnp.exp(cs_col), e)
    dx = _expand(jnp.exp(cs_last - cs_col) * dtc, e)
    elx = _expand(jnp.broadcast_to(jnp.exp(cs_last), (SUBLANES, LANES)), e)[0:1, :]
    seg = lax.broadcasted_iota(jnp.int32, (CHUNK, S_GROUP_WIDTH), 1) // S_HEAD_DIM
    yield

    for g in range(S_GROUPS):
        gcols = slice(g * S_GROUP_WIDTH, (g + 1) * S_GROUP_WIDTH)
        xs = conv(g * S_GROUP_WIDTH, S_GROUP_WIDTH)
        xs_b = xs.astype(BF16)
        b_g = conv(S_WIDTH + g * S_STATE, S_STATE).astype(BF16)
        c_g = conv(S_WIDTH + S_GROUPS * S_STATE + g * S_STATE, S_STATE).astype(BF16)
        yield
        cbm = _dot_nt(c_g, b_g)
        h_g = h_ref[:, gcols]
        y_off = _dot(c_g, h_g.astype(BF16)) * ex[:, gcols]
        ms, xst = [], []
        for r in range(S_HEADS_PER_GROUP):
            j = g * S_HEADS_PER_GROUP + r
            lmat = jnp.exp(jnp.where(causal, cs_col[:, j:j + 1] - cs_row[j:j + 1, :], -jnp.inf))
            ms.append((cbm * lmat * dt_t[j:j + 1, :]).astype(BF16))
            xst.append(jnp.where(seg == r, xs_b, jnp.zeros_like(xs_b)))
        y_diag = _dot(jnp.concatenate(ms, axis=1), jnp.concatenate(xst, axis=0))
        xd = (xs * dx[:, gcols]).astype(BF16)
        h_ref[:, gcols] = elx[:, gcols] * h_g + _dot_tn(b_g, xd)

        y = y_diag + y_off + prm["s_dskip"][:, gcols] * xs
        y = y * _silu(p[:, P_Z + g * S_GROUP_WIDTH:P_Z + (g + 1) * S_GROUP_WIDTH])
        ocols = slice(M_WIDTH + g * S_GROUP_WIDTH, M_WIDTH + (g + 1) * S_GROUP_WIDTH)
        cat_ref[rows, ocols] = _rms(y, prm["s_nrm"][:, gcols]).astype(cat_ref.dtype)
        yield


def _convmod_chunk(rows, p, prm, extc_ref, cat_ref):
    r0 = rows.start
    gate = _sigmoid(p[:, P_GLU + C_WIDTH:P_GLU + 2 * C_WIDTH])
    extc_ref[C_CARRY + r0:C_CARRY + r0 + CHUNK, :] = p[:, P_GLU:P_GLU + C_WIDTH] * gate
    yield
    pieces = []
    for lo in range(0, C_WIDTH, LANES):
        pieces.append(_causal_conv(extc_ref, r0, C_CARRY, C_KERNEL, prm["c_cw"], prm["c_cb"],
                                   slice(lo, lo + LANES)))
        yield
    acc = jnp.concatenate(pieces, axis=1)
    mu = jnp.mean(acc, axis=-1, keepdims=True)
    xc = acc - mu
    y = xc * lax.rsqrt(jnp.mean(xc * xc, axis=-1, keepdims=True) + LN_EPS)
    y = y * prm["c_lg"][...] + prm["c_lb"][...]
    cat_ref[rows, M_WIDTH + S_WIDTH:] = _silu(y).astype(cat_ref.dtype)


_MIXER_PARAMS = ("norm", "win", "wout", "m_cw", "m_cb", "m_gb", "m_nrm", "s_cw", "s_cb", "s_dtb",
                 "s_alog", "s_dskip", "s_nrm", "c_cw", "c_cb", "c_lg", "c_lb", "tri", "trit", "expand")


P_PIECE = 640


def _project(x, prm, p):
    hc = _rms(x, prm["norm"][...]).astype(BF16)
    for lo in range(0, P_COLS, P_PIECE):
        p[:, lo:lo + P_PIECE] = _dot(hc, prm["win"][:, lo:lo + P_PIECE])
        yield


def _round_robin(gens):
    gens = list(gens)
    while gens:
        for g in list(gens):
            try:
                next(g)
            except StopIteration:
                gens.remove(g)


def _mixer_kernel(x_ref, xn_ref, *refs):
    n_prm = len(_MIXER_PARAMS)
    prm = dict(zip(_MIXER_PARAMS, refs[:n_prm]))
    o_ref = refs[n_prm]
    extm_ref, exts_ref, extc_ref, c_ref, m_ref, h_ref, cat_ref, p_ref = refs[n_prm + 1:]
    tm = x_ref.shape[0]
    n_chunks = tm // CHUNK

    @pl.when((pl.program_id(0) == 0) & (pl.program_id(1) == 0))
    def _():
        for _ in _project(x_ref[0:CHUNK, :], prm, p_ref.at[0]):
            pass

    @pl.when(pl.program_id(1) == 0)
    def _():
        extm_ref[0:CONV_CARRY, :] = jnp.zeros((CONV_CARRY, extm_ref.shape[1]), F32)
        exts_ref[0:CONV_CARRY, :] = jnp.zeros((CONV_CARRY, exts_ref.shape[1]), F32)
        extc_ref[0:C_CARRY, :] = jnp.zeros((C_CARRY, extc_ref.shape[1]), F32)
        c_ref[...] = jnp.zeros(c_ref.shape, F32)
        m_ref[...] = jnp.full(m_ref.shape, M_INIT_LOG, F32)
        h_ref[...] = jnp.zeros(h_ref.shape, F32)

    causal = _causal_mask()
    for c in range(n_chunks):
        rows = slice(c * CHUNK, (c + 1) * CHUNK)
        x_next = x_ref[(c + 1) * CHUNK:(c + 2) * CHUNK, :] if c + 1 < n_chunks else xn_ref[...]
        p = p_ref.at[c % 2]
        _round_robin(
            [_mlstm_chunk(rows, p, prm, (extm_ref, c_ref, m_ref), cat_ref, causal),
             _project(x_next, prm, p_ref.at[(c + 1) % 2]),
             _ssd_chunk(rows, p, prm, (exts_ref, h_ref), cat_ref, causal),
             _convmod_chunk(rows, p, prm, extc_ref, cat_ref)])
        o_ref[rows, :] = x_ref[rows, :] + _dot(cat_ref[rows, :], prm["wout"][...])

    extm_ref[0:CONV_CARRY, :] = extm_ref[tm:tm + CONV_CARRY, :]
    exts_ref[0:CONV_CARRY, :] = exts_ref[tm:tm + CONV_CARRY, :]
    extc_ref[0:C_CARRY, :] = extc_ref[tm:tm + C_CARRY, :]


def _mixer(x, prm, batch, seq):
    d = x.shape[1]
    tm = TM_MIX
    nt = seq // tm
    params = [prm[k] for k in _MIXER_PARAMS]
    n_chunks = tm // CHUNK
    assert n_chunks % 2 == 0 and seq % tm == 0
    last_chunk = batch * seq // CHUNK - 1
    next_chunk = lambda b, i: (jnp.minimum((b * nt + i + 1) * n_chunks, last_chunk), 0)
    return pl.pallas_call(
        _mixer_kernel,
        name="mixer",
        grid=(batch, nt),
        in_specs=[pl.BlockSpec((tm, d), lambda b, i: (b * nt + i, 0)),
                  pl.BlockSpec((CHUNK, d), next_chunk)]
        + [_resident(p.shape) for p in params],
        out_specs=pl.BlockSpec((tm, d), lambda b, i: (b * nt + i, 0)),
        out_shape=jax.ShapeDtypeStruct((batch * seq, d), F32),
        scratch_shapes=[
            pltpu.VMEM((tm + CONV_CARRY, 2 * M_WIDTH), F32),
            pltpu.VMEM((tm + CONV_CARRY, XBC_WIDTH), F32),
            pltpu.VMEM((tm + C_CARRY, C_WIDTH), F32),
            pltpu.VMEM((M_HEADS, M_HEAD_DIM, 2 * M_HEAD_DIM), F32),
            pltpu.VMEM((M_HEADS, SUBLANES, LANES), F32),
            pltpu.VMEM((S_STATE, S_WIDTH), F32),
            pltpu.VMEM((tm, D_MIX), BF16),
            pltpu.VMEM((2, CHUNK, P_COLS), F32),
        ],
        compiler_params=pltpu.CompilerParams(
            dimension_semantics=("arbitrary", "arbitrary"), vmem_limit_bytes=VMEM_LIMIT),
    )(x, x, *params)


def _pad_lanes(a, width=LANES):
    return jnp.pad(a, ((0, 0), (0, width - a.shape[1])))


def _rearrange_w_in(w):
    o_if = 4 * M_WIDTH
    o_z = o_if + 2 * M_HEADS
    o_xbc = o_z + S_WIDTH
    o_dt = o_xbc + XBC_WIDTH
    o_glu = o_dt + S_HEADS
    return jnp.concatenate([
        w[:, 0:o_if], w[:, o_xbc:o_dt], w[:, o_z:o_xbc], w[:, o_glu:],
        _pad_lanes(w[:, o_if:o_z]), _pad_lanes(w[:, o_dt:o_glu])], axis=1).astype(BF16)


def kernel(x, ffn1_norm, ffn1_w_gate, ffn1_w_up, ffn1_w_down, mix_norm, w_in, w_out,
           mlstm_conv_w, mlstm_conv_b, mlstm_gate_b, mlstm_norm,
           ssd_conv_w, ssd_conv_b, ssd_dt_bias, ssd_a_log, ssd_d, ssd_norm,
           cm_conv_w, cm_conv_b, cm_ln_g, cm_ln_b,
           ffn2_norm, ffn2_w_gate, ffn2_w_up, ffn2_w_down, final_norm):
    batch, seq, d = x.shape
    depth = w_in.shape[0]
    xf = x.reshape(batch * seq, d)

    idx = jnp.arange(CHUNK)
    tri = (idx[None, :] <= idx[:, None]).astype(BF16)
    head_of_col = jnp.arange(S_WIDTH) // S_HEAD_DIM
    expand = (jnp.arange(LANES)[:, None] == head_of_col[None, :]).astype(BF16)
    row = lambda a: a.reshape(1, -1).astype(F32)
    fnorm = row(final_norm)

    for l in range(depth):
        xf = _ffn(xf, row(ffn1_norm[l]), ffn1_w_gate[l].astype(BF16), ffn1_w_up[l].astype(BF16),
                  ffn1_w_down[l].astype(BF16), fnorm, final=False)
        prm = dict(
            norm=row(mix_norm[l]), win=_rearrange_w_in(w_in[l]), wout=w_out[l].astype(BF16),
            m_cw=mlstm_conv_w[l], m_cb=row(mlstm_conv_b[l]), m_gb=_pad_lanes(row(mlstm_gate_b[l])),
            m_nrm=row(mlstm_norm[l]),
            s_cw=ssd_conv_w[l], s_cb=row(ssd_conv_b[l]), s_dtb=_pad_lanes(row(ssd_dt_bias[l])),
            s_alog=_pad_lanes(row(ssd_a_log[l])), s_dskip=row(jnp.repeat(ssd_d[l], S_HEAD_DIM)),
            s_nrm=row(ssd_norm[l]),
            c_cw=cm_conv_w[l], c_cb=row(cm_conv_b[l]), c_lg=row(cm_ln_g[l]), c_lb=row(cm_ln_b[l]),
            tri=tri, trit=tri.T, expand=expand)
        xf = _mixer(xf, prm, batch, seq)
        xf = _ffn(xf, row(ffn2_norm[l]), ffn2_w_gate[l].astype(BF16), ffn2_w_up[l].astype(BF16),
                  ffn2_w_down[l].astype(BF16), fnorm, final=(l == depth - 1))
    return xf.reshape(batch, seq, d)
```

```python
import functools

import jax
import jax.numpy as jnp
from jax import lax
from jax.experimental import pallas as pl
from jax.experimental.pallas import tpu as pltpu

F32 = jnp.float32
BF16 = jnp.bfloat16

RMS_EPS = 1e-6
LN_EPS = 1e-5
M_INIT_LOG = -1e30

CHUNK = 128
LANES = 128
SUBLANES = 8
M_HEADS = 4
M_HEAD_DIM = 128
M_WIDTH = M_HEADS * M_HEAD_DIM
M_CONV = 4
S_HEADS = 16
S_HEAD_DIM = 64
S_GROUPS = 4
S_STATE = 128
S_WIDTH = S_HEADS * S_HEAD_DIM
S_GROUP_WIDTH = S_WIDTH // S_GROUPS
S_HEADS_PER_GROUP = S_HEADS // S_GROUPS
S_CONV = 4
XBC_WIDTH = S_WIDTH + 2 * S_GROUPS * S_STATE
C_WIDTH = 512
C_KERNEL = 31
CONV_CARRY = SUBLANES
C_CARRY = 32
D_MIX = M_WIDTH + S_WIDTH + C_WIDTH

P_QKVO = 0
P_XBC = 2048
P_Z = 4096
P_GLU = 5120
P_GATES = 6144
P_COLS = 6400

TM_DENSE = 512
TM_MIX = 512
VMEM_LIMIT = 56 * 1024 * 1024


def _sigmoid(x):
    return 1.0 / (1.0 + jnp.exp(-x))


def _silu(x):
    return x * _sigmoid(x)


def _softplus(x):
    return jnp.maximum(x, 0.0) + jnp.log1p(jnp.exp(-jnp.abs(x)))


def _log_sigmoid(x):
    return jnp.minimum(x, 0.0) - jnp.log1p(jnp.exp(-jnp.abs(x)))


def _rms(x, g):
    return x * lax.rsqrt(jnp.mean(x * x, axis=-1, keepdims=True) + RMS_EPS) * g


def _dot(a, b):
    return jnp.dot(a, b, preferred_element_type=F32)


def _dot_nt(a, b):
    return lax.dot_general(a, b, (((1,), (1,)), ((), ())), preferred_element_type=F32)


def _dot_tn(a, b):
    return lax.dot_general(a, b, (((0,), (0,)), ((), ())), preferred_element_type=F32)


def _split3(a):
    hi = a.astype(BF16)
    r = a - hi.astype(F32)
    mid = r.astype(BF16)
    lo = (r - mid.astype(F32)).astype(BF16)
    return hi, mid, lo


def _dot_split_rhs(a_bf16, b_f32):
    hi, mid, lo = _split3(b_f32)
    return _dot(a_bf16, hi) + _dot(a_bf16, mid) + _dot(a_bf16, lo)


def _dot_split_lhs(a_f32, b_bf16):
    hi, mid, lo = _split3(a_f32)
    return _dot(hi, b_bf16) + _dot(mid, b_bf16) + _dot(lo, b_bf16)


def _expand(v, e):
    hi = v.astype(BF16)
    lo = (v - hi.astype(F32)).astype(BF16)
    return _dot(hi, e) + _dot(lo, e)


def _causal_mask():
    t = lax.broadcasted_iota(jnp.int32, (CHUNK, CHUNK), 0)
    s = lax.broadcasted_iota(jnp.int32, (CHUNK, CHUNK), 1)
    return s <= t


def _causal_conv(ext_ref, r0, carry, width, w_ref, b_ref, cols):
    win = ext_ref[r0:r0 + carry + CHUNK, cols]
    acc = b_ref[:, cols]
    for k in range(SUBLANES):
        taps = [j for j in range(width) if (width - 1 - j) % SUBLANES == k]
        if not taps:
            continue
        rolled = win if k == 0 else pltpu.roll(win, k, 0)
        for j in taps:
            back = width - 1 - j
            start = carry - (back - k)
            acc = acc + w_ref[j:j + 1, cols] * rolled[start:start + CHUNK, :]
    return acc


def _ffn_kernel(x_ref, g_ref, wg_ref, wu_ref, wd_ref, fg_ref, o_ref, *, final):
    x = x_ref[...]
    h = _rms(x, g_ref[...]).astype(BF16)
    a = _dot(h, wg_ref[...])
    b = _dot(h, wu_ref[...])
    t = (_silu(a) * b).astype(BF16)
    y = x + 0.5 * _dot(t, wd_ref[...])
    if final:
        y = _rms(y, fg_ref[...])
    o_ref[...] = y


def _resident(shape):
    return pl.BlockSpec(shape, lambda *_: (0,) * len(shape), pipeline_mode=pl.Buffered(1))


def _layer(a, l):
    return pl.BlockSpec((None,) + a.shape[1:], lambda *_: (l,) + (0,) * (a.ndim - 1),
                        pipeline_mode=pl.Buffered(1))


def _ffn(x, l, g, wg, wu, wd, fg, final):
    t, d = x.shape
    tm = TM_DENSE
    return pl.pallas_call(
        functools.partial(_ffn_kernel, final=final),
        name="ffn_final" if final else "ffn",
        grid=(t // tm,),
        in_specs=[
            pl.BlockSpec((tm, d), lambda i: (i, 0)),
            _layer(g, l), _layer(wg, l), _layer(wu, l), _layer(wd, l),
            _resident(fg.shape),
        ],
        out_specs=pl.BlockSpec((tm, d), lambda i: (i, 0)),
        out_shape=jax.ShapeDtypeStruct((t, d), F32),
        compiler_params=pltpu.CompilerParams(
            dimension_semantics=("arbitrary",), vmem_limit_bytes=VMEM_LIMIT),
    )(x, g, wg, wu, wd, fg)


def _mlstm_chunk(rows, p, prm, scr, cat_ref, causal):
    extm_ref, c_ref, m_ref = scr
    r0 = rows.start
    extm_ref[CONV_CARRY + r0:CONV_CARRY + r0 + CHUNK, :] = p[:, P_QKVO:P_QKVO + 2 * M_WIDTH]
    conv = lambda cols: _silu(_causal_conv(extm_ref, r0, CONV_CARRY, M_CONV, prm["m_cw"], prm["m_cb"], cols))
    q_all = conv(slice(0, M_WIDTH)).astype(BF16)
    yield
    k_all = conv(slice(M_WIDTH, 2 * M_WIDTH)) * (M_HEAD_DIM ** -0.5)
    yield

    gg = p[:, P_GATES:P_GATES + LANES] + prm["m_gb"][...]
    lane = lax.broadcasted_iota(jnp.int32, gg.shape, 1)
    gc = jnp.where((lane >= M_HEADS) & (lane < 2 * M_HEADS), _log_sigmoid(gg), gg)
    gt = gc.T
    b_cols = _dot_split_rhs(prm["tri"][...], gc)
    b_rows = _dot_split_lhs(gt[0:SUBLANES, :], prm["trit"][...])
    ones = jnp.ones((CHUNK, M_HEAD_DIM), BF16)
    yield

    for h in range(M_HEADS):
        cols = slice(h * M_HEAD_DIM, (h + 1) * M_HEAD_DIM)
        q_h = q_all[:, cols]
        k_h = k_all[:, cols]
        v_off = P_QKVO + 2 * M_WIDTH + h * M_HEAD_DIM
        o_off = P_QKVO + 3 * M_WIDTH + h * M_HEAD_DIM
        v_h = p[:, v_off:v_off + M_HEAD_DIM]
        o_h = p[:, o_off:o_off + M_HEAD_DIM]
        b_col = b_cols[:, M_HEADS + h:M_HEADS + h + 1]
        i_col = gc[:, h:h + 1]
        b_row = b_rows[M_HEADS + h:M_HEADS + h + 1, :]
        i_row = gt[h:h + 1, :]
        m = m_ref[h, 0:1, 0:1]
        g = b_col[CHUNK - 1:CHUNK, :]

        dmat = jnp.where(causal, (b_col - b_row) + i_row, -jnp.inf)
        inter = b_col + m
        m_t = jnp.maximum(inter, jnp.max(dmat, axis=1, keepdims=True))
        w = jnp.exp(dmat - m_t)
        sqk = (_dot_nt(q_h, k_h.astype(BF16)) * w).astype(BF16)
        a = jnp.exp(inter - m_t)
        v_aug = jnp.concatenate([v_h.astype(BF16), ones], axis=1)
        c_aug = c_ref[h]
        res = _dot(sqk, v_aug) + a * _dot(q_h, c_aug.astype(BF16))
        num = res[:, :M_HEAD_DIM]
        den = res[:, M_HEAD_DIM:M_HEAD_DIM + 1]
        hh = num / jnp.maximum(jnp.abs(den), jnp.exp(-m_t))

        u = (g - b_col) + i_col
        m_new = jnp.maximum(g + m, jnp.max(u, axis=0, keepdims=True))
        ws = jnp.exp(u - m_new)
        decay = jnp.exp(g + m - m_new)
        kw = (k_h * ws).astype(BF16)
        c_ref[h] = decay * c_aug + _dot_tn(kw, v_aug)
        m_ref[h] = jnp.broadcast_to(m_new, m_ref.shape[1:])

        out = _sigmoid(o_h) * hh
        cat_ref[rows, cols] = _rms(out, prm["m_nrm"][:, cols]).astype(cat_ref.dtype)
        yield


def _ssd_chunk(rows, p, prm, scr, cat_ref, causal):
    exts_ref, h_ref = scr
    r0 = rows.start
    exts_ref[CONV_CARRY + r0:CONV_CARRY + r0 + CHUNK, :] = p[:, P_XBC:P_XBC + XBC_WIDTH]
    conv = lambda off, n: _silu(_causal_conv(exts_ref, r0, CONV_CARRY, S_CONV, prm["s_cw"], prm["s_cb"],
                                             slice(off, off + n)))

    dtv =_softplus(p[:, P_GATES + LANES:P_COLS] + prm["s_dtb"][...])
    lane = lax.broadcasted_iota(jnp.int32, dtv.shape, 1)
    dtc = jnp.where(lane < S_HEADS, dtv, 0.0)
    adt = dtc * (-jnp.exp(prm["s_alog"][...]))
    adt_t = adt.T
    dt_t = dtc.T
    e = prm["expand"][...]
    cs_col = _dot_split_rhs(prm["tri"][...], adt)
    cs_row = _dot_split_lhs(adt_t[0:S_HEADS, :], prm["trit"][...])
    cs_last = cs_col[CHUNK - 1:CHUNK, :]
    ex = _expand(jnp.exp(cs_col), e)
    dx = _expand(jnp.exp(cs_last - cs_col) * dtc, e)
    elx = _expand(jnp.broadcast_to(jnp.exp(cs_last), (SUBLANES, LANES)), e)[0:1, :]
    seg = lax.broadcasted_iota(jnp.int32, (CHUNK, S_GROUP_WIDTH), 1) // S_HEAD_DIM
    yield

    for g in range(S_GROUPS):
        gcols = slice(g * S_GROUP_WIDTH, (g + 1) * S_GROUP_WIDTH)
        xs = conv(g * S_GROUP_WIDTH, S_GROUP_WIDTH)
        xs_b = xs.astype(BF16)
        b_g = conv(S_WIDTH + g * S_STATE, S_STATE).astype(BF16)
        c_g = conv(S_WIDTH + S_GROUPS * S_STATE + g * S_STATE, S_STATE).astype(BF16)
        yield
        cbm = _dot_nt(c_g, b_g)
        h_g = h_ref[:, gcols]
        y_off = _dot(c_g, h_g.astype(BF16)) * ex[:, gcols]
        ms, xst = [], []
        for r in range(S_HEADS_PER_GROUP):
            j = g * S_HEADS_PER_GROUP + r
            lmat = jnp.exp(jnp.where(causal, cs_col[:, j:j + 1] - cs_row[j:j + 1, :], -jnp.inf))
            ms.append((cbm * lmat * dt_t[j:j + 1, :]).astype(BF16))
            xst.append(jnp.where(seg == r, xs_b, jnp.zeros_like(xs_b)))
        y_diag = _dot(jnp.concatenate(ms, axis=1), jnp.concatenate(xst, axis=0))
        xd = (xs * dx[:, gcols]).astype(BF16)
        h_ref[:, gcols] = elx[:, gcols] * h_g + _dot_tn(b_g, xd)

        y = y_diag + y_off + prm["s_dskip"][:, gcols] * xs
        y = y * _silu(p[:, P_Z + g * S_GROUP_WIDTH:P_Z + (g + 1) * S_GROUP_WIDTH])
        ocols = slice(M_WIDTH + g * S_GROUP_WIDTH, M_WIDTH + (g + 1) * S_GROUP_WIDTH)
        cat_ref[rows, ocols] = _rms(y, prm["s_nrm"][:, gcols]).astype(cat_ref.dtype)
        yield


def _convmod_chunk(rows, p, prm, extc_ref, cat_ref):
    r0 = rows.start
    gate = _sigmoid(p[:, P_GLU + C_WIDTH:P_GLU + 2 * C_WIDTH])
    extc_ref[C_CARRY + r0:C_CARRY + r0 + CHUNK, :] = p[:, P_GLU:P_GLU + C_WIDTH] * gate
    yield
    pieces = []
    for lo in range(0, C_WIDTH, LANES):
        pieces.append(_causal_conv(extc_ref, r0, C_CARRY, C_KERNEL, prm["c_cw"], prm["c_cb"],
                                   slice(lo, lo + LANES)))
        yield
    acc = jnp.concatenate(pieces, axis=1)
    mu = jnp.mean(acc, axis=-1, keepdims=True)
    xc = acc - mu
    y = xc * lax.rsqrt(jnp.mean(xc * xc, axis=-1, keepdims=True) + LN_EPS)
    y = y * prm["c_lg"][...] + prm["c_lb"][...]
    cat_ref[rows, M_WIDTH + S_WIDTH:] = _silu(y).astype(cat_ref.dtype)


_MIXER_PARAMS = ("norm", "win", "wout", "m_cw", "m_cb", "m_gb", "m_nrm", "s_cw", "s_cb", "s_dtb",
                 "s_alog", "s_dskip", "s_nrm", "c_cw", "c_cb", "c_lg", "c_lb", "tri", "trit", "expand")


P_PIECE = 256


def _project(x, prm, p):
    hc = _rms(x, prm["norm"][...]).astype(BF16)
    for lo in range(0, P_COLS, P_PIECE):
        p[:, lo:lo + P_PIECE] = _dot(hc, prm["win"][:, lo:lo + P_PIECE])
        yield


def _round_robin(gens):
    gens = list(gens)
    while gens:
        for g in list(gens):
            try:
                next(g)
            except StopIteration:
                gens.remove(g)


def _mixer_kernel(x_ref, xn_ref, *refs):
    n_prm = len(_MIXER_PARAMS)
    prm = dict(zip(_MIXER_PARAMS, refs[:n_prm]))
    o_ref = refs[n_prm]
    extm_ref, exts_ref, extc_ref, c_ref, m_ref, h_ref, cat_ref, p_ref = refs[n_prm + 1:]
    tm = x_ref.shape[0]
    n_chunks = tm // CHUNK

    @pl.when((pl.program_id(0) == 0) & (pl.program_id(1) == 0))
    def _():
        for _ in _project(x_ref[0:CHUNK, :], prm, p_ref.at[0]):
            pass

    @pl.when(pl.program_id(1) == 0)
    def _():
        extm_ref[0:CONV_CARRY, :] = jnp.zeros((CONV_CARRY, extm_ref.shape[1]), F32)
        exts_ref[0:CONV_CARRY, :] = jnp.zeros((CONV_CARRY, exts_ref.shape[1]), F32)
        extc_ref[0:C_CARRY, :] = jnp.zeros((C_CARRY, extc_ref.shape[1]), F32)
        c_ref[...] = jnp.zeros(c_ref.shape, F32)
        m_ref[...] = jnp.full(m_ref.shape, M_INIT_LOG, F32)
        h_ref[...] = jnp.zeros(h_ref.shape, F32)

    causal = _causal_mask()
    for c in range(n_chunks):
        rows = slice(c * CHUNK, (c + 1) * CHUNK)
        x_next = x_ref[(c + 1) * CHUNK:(c + 2) * CHUNK, :] if c + 1 < n_chunks else xn_ref[...]
        p = p_ref.at[c % 2]
        proj = _project(x_next, prm, p_ref.at[(c + 1) % 2])
        _round_robin(
            [_mlstm_chunk(rows, p, prm, (extm_ref, c_ref, m_ref), cat_ref, causal), proj,
             _ssd_chunk(rows, p, prm, (exts_ref, h_ref), cat_ref, causal), proj,
             _convmod_chunk(rows, p, prm, extc_ref, cat_ref), proj])
        o_ref[rows, :] = x_ref[rows, :] + _dot(cat_ref[rows, :], prm["wout"][...])

    extm_ref[0:CONV_CARRY, :] = extm_ref[tm:tm + CONV_CARRY, :]
    exts_ref[0:CONV_CARRY, :] = exts_ref[tm:tm + CONV_CARRY, :]
    extc_ref[0:C_CARRY, :] = extc_ref[tm:tm + C_CARRY, :]


_MIXER_SHARED = ("tri", "trit", "expand")


def _mixer(x, l, prm, batch, seq):
    d = x.shape[1]
    tm = TM_MIX
    nt = seq // tm
    params = [prm[k] for k in _MIXER_PARAMS]
    param_specs = [_resident(prm[k].shape) if k in _MIXER_SHARED else _layer(prm[k], l)
                   for k in _MIXER_PARAMS]
    n_chunks = tm // CHUNK
    assert n_chunks % 2 == 0 and seq % tm == 0
    last_chunk = batch * seq // CHUNK - 1
    next_chunk = lambda b, i: (jnp.minimum((b * nt + i + 1) * n_chunks, last_chunk), 0)
    return pl.pallas_call(
        _mixer_kernel,
        name="mixer",
        grid=(batch, nt),
        in_specs=[pl.BlockSpec((tm, d), lambda b, i: (b * nt + i, 0)),
                  pl.BlockSpec((CHUNK, d), next_chunk)]
        + param_specs,
        out_specs=pl.BlockSpec((tm, d), lambda b, i: (b * nt + i, 0)),
        out_shape=jax.ShapeDtypeStruct((batch * seq, d), F32),
        scratch_shapes=[
            pltpu.VMEM((tm + CONV_CARRY, 2 * M_WIDTH), F32),
            pltpu.VMEM((tm + CONV_CARRY, XBC_WIDTH), F32),
            pltpu.VMEM((tm + C_CARRY, C_WIDTH), F32),
            pltpu.VMEM((M_HEADS, M_HEAD_DIM, 2 * M_HEAD_DIM), F32),
            pltpu.VMEM((M_HEADS, SUBLANES, LANES), F32),
            pltpu.VMEM((S_STATE, S_WIDTH), F32),
            pltpu.VMEM((tm, D_MIX), BF16),
            pltpu.VMEM((2, CHUNK, P_COLS), F32),
        ],
        compiler_params=pltpu.CompilerParams(
            dimension_semantics=("arbitrary", "arbitrary"), vmem_limit_bytes=VMEM_LIMIT),
    )(x, x, *params)


def _pad_lanes(a, width=LANES):
    return jnp.pad(a, [(0, 0)] * (a.ndim - 1) + [(0, width - a.shape[-1])])


def _rearrange_w_in(w):
    o_if = 4 * M_WIDTH
    o_z = o_if + 2 * M_HEADS
    o_xbc = o_z + S_WIDTH
    o_dt = o_xbc + XBC_WIDTH
    o_glu = o_dt + S_HEADS
    return jnp.concatenate([
        w[..., 0:o_if], w[..., o_xbc:o_dt], w[..., o_z:o_xbc], w[..., o_glu:],
        _pad_lanes(w[..., o_if:o_z]), _pad_lanes(w[..., o_dt:o_glu])], axis=-1).astype(BF16)


def kernel(x, ffn1_norm, ffn1_w_gate, ffn1_w_up, ffn1_w_down, mix_norm, w_in, w_out,
           mlstm_conv_w, mlstm_conv_b, mlstm_gate_b, mlstm_norm,
           ssd_conv_w, ssd_conv_b, ssd_dt_bias, ssd_a_log, ssd_d, ssd_norm,
           cm_conv_w, cm_conv_b, cm_ln_g, cm_ln_b,
           ffn2_norm, ffn2_w_gate, ffn2_w_up, ffn2_w_down, final_norm):
    batch, seq, d = x.shape
    depth = w_in.shape[0]
    xf = x.reshape(batch * seq, d)

    idx = jnp.arange(CHUNK)
    tri = (idx[None, :] <= idx[:, None]).astype(BF16)
    head_of_col = jnp.arange(S_WIDTH) // S_HEAD_DIM
    expand = (jnp.arange(LANES)[:, None] == head_of_col[None, :]).astype(BF16)
    rows = lambda a: a.reshape(depth, 1, -1).astype(F32)
    bf = lambda a: a.astype(BF16)
    fnorm = final_norm.reshape(1, -1).astype(F32)
    ffn1 = (rows(ffn1_norm), bf(ffn1_w_gate), bf(ffn1_w_up), bf(ffn1_w_down))
    ffn2 = (rows(ffn2_norm), bf(ffn2_w_gate), bf(ffn2_w_up), bf(ffn2_w_down))
    prm = dict(
        norm=rows(mix_norm), win=_rearrange_w_in(w_in), wout=bf(w_out),
        m_cw=mlstm_conv_w, m_cb=rows(mlstm_conv_b), m_gb=_pad_lanes(rows(mlstm_gate_b)),
        m_nrm=rows(mlstm_norm),
        s_cw=ssd_conv_w, s_cb=rows(ssd_conv_b), s_dtb=_pad_lanes(rows(ssd_dt_bias)),
        s_alog=_pad_lanes(rows(ssd_a_log)), s_dskip=rows(jnp.repeat(ssd_d, S_HEAD_DIM, axis=1)),
        s_nrm=rows(ssd_norm),
        c_cw=cm_conv_w, c_cb=rows(cm_conv_b), c_lg=rows(cm_ln_g), c_lb=rows(cm_ln_b),
        tri=tri, trit=tri.T, expand=expand)

    for l in range(depth):
        xf = _ffn(xf, l, *ffn1, fnorm, final=False)
        xf = _mixer(xf, l, prm, batch, seq)
        xf = _ffn(xf, l, *ffn2, fnorm, final=(l == depth - 1))
    return xf.reshape(batch, seq, d)
```

```python
import functools

import jax
import jax.numpy as jnp
from jax import lax
from jax.experimental import pallas as pl
from jax.experimental.pallas import tpu as pltpu

F32 = jnp.float32
BF16 = jnp.bfloat16

RMS_EPS = 1e-6
LN_EPS = 1e-5
M_INIT_LOG = -1e30

CHUNK = 128
LANES = 128
SUBLANES = 8
M_HEADS = 4
M_HEAD_DIM = 128
M_WIDTH = M_HEADS * M_HEAD_DIM
M_CONV = 4
S_HEADS = 16
S_HEAD_DIM = 64
S_GROUPS = 4
S_STATE = 128
S_WIDTH = S_HEADS * S_HEAD_DIM
S_GROUP_WIDTH = S_WIDTH // S_GROUPS
S_HEADS_PER_GROUP = S_HEADS // S_GROUPS
S_CONV = 4
XBC_WIDTH = S_WIDTH + 2 * S_GROUPS * S_STATE
C_WIDTH = 512
C_KERNEL = 31
CONV_CARRY = SUBLANES
C_CARRY = 32
D_MIX = M_WIDTH + S_WIDTH + C_WIDTH

P_QKVO = 0
P_XBC = 2048
P_Z = 4096
P_GLU = 5120
P_GATES = 6144
P_COLS = 6400

TM_DENSE = 512
TM_MIX = 512
VMEM_LIMIT = 56 * 1024 * 1024


def _sigmoid(x):
    return 1.0 / (1.0 + jnp.exp(-x))


def _silu(x):
    return x * _sigmoid(x)


def _softplus(x):
    return jnp.maximum(x, 0.0) + jnp.log1p(jnp.exp(-jnp.abs(x)))


def _log_sigmoid(x):
    return jnp.minimum(x, 0.0) - jnp.log1p(jnp.exp(-jnp.abs(x)))


def _rms(x, g):
    return x * lax.rsqrt(jnp.mean(x * x, axis=-1, keepdims=True) + RMS_EPS) * g


def _dot(a, b):
    return jnp.dot(a, b, preferred_element_type=F32)


def _dot_nt(a, b):
    return lax.dot_general(a, b, (((1,), (1,)), ((), ())), preferred_element_type=F32)


def _dot_tn(a, b):
    return lax.dot_general(a, b, (((0,), (0,)), ((), ())), preferred_element_type=F32)


def _split3(a):
    hi = a.astype(BF16)
    r = a - hi.astype(F32)
    mid = r.astype(BF16)
    lo = (r - mid.astype(F32)).astype(BF16)
    return hi, mid, lo


def _dot_split_rhs(a_bf16, b_f32):
    hi, mid, lo = _split3(b_f32)
    return _dot(a_bf16, hi) + _dot(a_bf16, mid) + _dot(a_bf16, lo)


def _dot_split_lhs(a_f32, b_bf16):
    hi, mid, lo = _split3(a_f32)
    return _dot(hi, b_bf16) + _dot(mid, b_bf16) + _dot(lo, b_bf16)


def _expand(v, e):
    hi = v.astype(BF16)
    lo = (v - hi.astype(F32)).astype(BF16)
    return _dot(hi, e) + _dot(lo, e)


def _causal_mask():
    t = lax.broadcasted_iota(jnp.int32, (CHUNK, CHUNK), 0)
    s = lax.broadcasted_iota(jnp.int32, (CHUNK, CHUNK), 1)
    return s <= t


def _causal_conv(ext_ref, r0, carry, width, w_ref, b_ref, cols):
    win = ext_ref[r0:r0 + carry + CHUNK, cols]
    acc = b_ref[:, cols]
    for k in range(SUBLANES):
        taps = [j for j in range(width) if (width - 1 - j) % SUBLANES == k]
        if not taps:
            continue
        rolled = win if k == 0 else pltpu.roll(win, k, 0)
        for j in taps:
            back = width - 1 - j
            start = carry - (back - k)
            acc = acc + w_ref[j:j + 1, cols] * rolled[start:start + CHUNK, :]
    return acc


def _ffn_kernel(x_ref, g_ref, wg_ref, wu_ref, wd_ref, ng_ref, o_ref, *maybe_h_ref, post):
    x = x_ref[...]
    h = _rms(x, g_ref[...]).astype(BF16)
    a = _dot(h, wg_ref[...])
    b = _dot(h, wu_ref[...])
    t = (_silu(a) * b).astype(BF16)
    y = x + 0.5 * _dot(t, wd_ref[...])
    if post == "final":
        y = _rms(y, ng_ref[...])
    o_ref[...] = y
    if post == "normed":
        maybe_h_ref[0][...] = _rms(y, ng_ref[...]).astype(BF16)


def _resident(shape):
    return pl.BlockSpec(shape, lambda *_: (0,) * len(shape), pipeline_mode=pl.Buffered(1))


def _layer(a, l):
    return pl.BlockSpec((None,) + a.shape[1:], lambda *_: (l,) + (0,) * (a.ndim - 1),
                        pipeline_mode=pl.Buffered(1))


def _ffn(x, l, g, wg, wu, wd, ng_spec, ng, post):
    t, d = x.shape
    tm = TM_DENSE
    tile = pl.BlockSpec((tm, d), lambda i: (i, 0))
    out_specs, out_shape = tile, jax.ShapeDtypeStruct((t, d), F32)
    if post == "normed":
        out_specs, out_shape = (tile, tile), (out_shape, jax.ShapeDtypeStruct((t, d), BF16))
    return pl.pallas_call(
        functools.partial(_ffn_kernel, post=post),
        name="ffn_" + str(post).lower(),
        grid=(t // tm,),
        in_specs=[tile, _layer(g, l), _layer(wg, l), _layer(wu, l), _layer(wd, l), ng_spec],
        out_specs=out_specs,
        out_shape=out_shape,
        compiler_params=pltpu.CompilerParams(
            dimension_semantics=("arbitrary",), vmem_limit_bytes=VMEM_LIMIT),
    )(x, g, wg, wu, wd, ng)


def _mlstm_chunk(rows, p, prm, scr, cat_ref, causal):
    extm_ref, c_ref, m_ref = scr
    r0 = rows.start
    extm_ref[CONV_CARRY + r0:CONV_CARRY + r0 + CHUNK, :] = p[:, P_QKVO:P_QKVO + 2 * M_WIDTH]
    conv = lambda cols: _silu(_causal_conv(extm_ref, r0, CONV_CARRY, M_CONV, prm["m_cw"], prm["m_cb"], cols))
    q_all = conv(slice(0, M_WIDTH))
    yield
    k_all = conv(slice(M_WIDTH, 2 * M_WIDTH)) * (M_HEAD_DIM ** -0.5)
    yield

    gg = p[:, P_GATES:P_GATES + LANES].T[0:2 * M_HEADS, :] + prm["m_gbt"][...]
    row = lax.broadcasted_iota(jnp.int32, gg.shape, 0)
    gt = jnp.where(row >= M_HEADS, _log_sigmoid(gg), gg)
    gc = jnp.concatenate([gt, jnp.zeros((LANES - 2 * M_HEADS, CHUNK), F32)], axis=0).T
    b_cols = _dot_split_rhs(prm["tri"][...], gc)
    b_rows = _dot_split_lhs(gt, prm["trit"][...])
    ones = jnp.ones((CHUNK, M_HEAD_DIM), BF16)
    yield

    for h in range(M_HEADS):
        cols = slice(h * M_HEAD_DIM, (h + 1) * M_HEAD_DIM)
        q_h = q_all[:, cols]
        k_h = k_all[:, cols]
        v_off = P_QKVO + 2 * M_WIDTH + h * M_HEAD_DIM
        o_off = P_QKVO + 3 * M_WIDTH + h * M_HEAD_DIM
        v_h = p[:, v_off:v_off + M_HEAD_DIM]
        o_h = p[:, o_off:o_off + M_HEAD_DIM]
        b_col = b_cols[:, M_HEADS + h:M_HEADS + h + 1]
        i_col = gc[:, h:h + 1]
        b_row = b_rows[M_HEADS + h:M_HEADS + h + 1, :]
        i_row = gt[h:h + 1, :]
        m = m_ref[h, 0:1, 0:1]
        g = b_col[CHUNK - 1:CHUNK, :]

        dmat = jnp.where(causal, (b_col - b_row) + i_row, -jnp.inf)
        inter = b_col + m
        m_t = jnp.maximum(inter, jnp.max(dmat, axis=1, keepdims=True))
        w = jnp.exp(dmat - m_t)
        sqk = (_dot_nt(q_h.astype(BF16), k_h.astype(BF16)) * w).astype(BF16)
        a = jnp.exp(inter - m_t)
        v_aug = jnp.concatenate([v_h.astype(BF16), ones], axis=1)
        c_aug = c_ref[h]
        lhs = jnp.concatenate([sqk, (a * q_h).astype(BF16)], axis=1)
        res = _dot(lhs, jnp.concatenate([v_aug, c_aug.astype(BF16)], axis=0))
        num = res[:, :M_HEAD_DIM]
        den = res[:, M_HEAD_DIM:M_HEAD_DIM + 1]
        hh = num / jnp.maximum(jnp.abs(den), jnp.exp(-m_t))

        u = (g - b_col) + i_col
        m_new = jnp.maximum(g + m, jnp.max(u, axis=0, keepdims=True))
        ws = jnp.exp(u - m_new)
        decay = jnp.exp(g + m - m_new)
        kw = (k_h * ws).astype(BF16)
        c_ref[h] = decay * c_aug + _dot_tn(kw, v_aug)
        m_ref[h] = jnp.broadcast_to(m_new, m_ref.shape[1:])

        out = _sigmoid(o_h) * hh
        cat_ref[rows, cols] = _rms(out, prm["m_nrm"][:, cols]).astype(cat_ref.dtype)
        yield


def _ssd_chunk(rows, p, prm, scr, cat_ref, causal):
    exts_ref, h_ref = scr
    r0 = rows.start
    exts_ref[CONV_CARRY + r0:CONV_CARRY + r0 + CHUNK, :] = p[:, P_XBC:P_XBC + XBC_WIDTH]
    conv = lambda off, n: _silu(_causal_conv(exts_ref, r0, CONV_CARRY, S_CONV, prm["s_cw"], prm["s_cb"],
                                             slice(off, off + n)))

    dt_t = _softplus(p[:, P_GATES + LANES:P_COLS].T[0:S_HEADS, :] + prm["s_dtbt"][...])
    adt_t = dt_t * (-jnp.exp(prm["s_alogt"][...]))
    dtc = jnp.concatenate([dt_t, jnp.zeros((LANES - S_HEADS, CHUNK), F32)], axis=0).T
    adt = dtc * (-jnp.exp(prm["s_alog"][...]))
    e = prm["expand"][...]
    cs_col = _dot_split_rhs(prm["tri"][...], adt)
    cs_row = _dot_split_lhs(adt_t, prm["trit"][...])
    cs_last = cs_col[CHUNK - 1:CHUNK, :]
    ex = _dot(jnp.exp(cs_col).astype(BF16), e)
    dx = _dot((jnp.exp(cs_last - cs_col) * dtc).astype(BF16), e)
    elx = _expand(jnp.broadcast_to(jnp.exp(cs_last), (SUBLANES, LANES)), e)[0:1, :]
    seg = lax.broadcasted_iota(jnp.int32, (CHUNK, S_GROUP_WIDTH), 1) // S_HEAD_DIM
    yield

    for g in range(S_GROUPS):
        gcols = slice(g * S_GROUP_WIDTH, (g + 1) * S_GROUP_WIDTH)
        xs = conv(g * S_GROUP_WIDTH, S_GROUP_WIDTH)
        xs_b = xs.astype(BF16)
        b_g = conv(S_WIDTH + g * S_STATE, S_STATE).astype(BF16)
        c_g = conv(S_WIDTH + S_GROUPS * S_STATE + g * S_STATE, S_STATE).astype(BF16)
        yield
        cbm = _dot_nt(c_g, b_g)
        h_g = h_ref[:, gcols]
        y_off = _dot(c_g, h_g.astype(BF16)) * ex[:, gcols]
        ms, xst = [], []
        for r in range(S_HEADS_PER_GROUP):
            j = g * S_HEADS_PER_GROUP + r
            lmat = jnp.exp(jnp.where(causal, cs_col[:, j:j + 1] - cs_row[j:j + 1, :], -jnp.inf))
            ms.append((cbm * lmat * dt_t[j:j + 1, :]).astype(BF16))
            xst.append(jnp.where(seg == r, xs_b, jnp.zeros_like(xs_b)))
        y_diag = _dot(jnp.concatenate(ms, axis=1), jnp.concatenate(xst, axis=0))
        xd = (xs * dx[:, gcols]).astype(BF16)
        h_ref[:, gcols] = elx[:, gcols] * h_g + _dot_tn(b_g, xd)

        y = y_diag + y_off + prm["s_dskip"][:, gcols] * xs
        y = y * _silu(p[:, P_Z + g * S_GROUP_WIDTH:P_Z + (g + 1) * S_GROUP_WIDTH])
        ocols = slice(M_WIDTH + g * S_GROUP_WIDTH, M_WIDTH + (g + 1) * S_GROUP_WIDTH)
        cat_ref[rows, ocols] = _rms(y, prm["s_nrm"][:, gcols]).astype(cat_ref.dtype)
        yield


def _convmod_chunk(rows, p, prm, extc_ref, cat_ref):
    r0 = rows.start
    gate = _sigmoid(p[:, P_GLU + C_WIDTH:P_GLU + 2 * C_WIDTH])
    extc_ref[C_CARRY + r0:C_CARRY + r0 + CHUNK, :] = p[:, P_GLU:P_GLU + C_WIDTH] * gate
    yield
    pieces = []
    for lo in range(0, C_WIDTH, LANES):
        pieces.append(_causal_conv(extc_ref, r0, C_CARRY, C_KERNEL, prm["c_cw"], prm["c_cb"],
                                   slice(lo, lo + LANES)))
        yield
    acc = jnp.concatenate(pieces, axis=1)
    mu = jnp.mean(acc, axis=-1, keepdims=True)
    xc = acc - mu
    y = xc * lax.rsqrt(jnp.mean(xc * xc, axis=-1, keepdims=True) + LN_EPS)
    y = y * prm["c_lg"][...] + prm["c_lb"][...]
    cat_ref[rows, M_WIDTH + S_WIDTH:] = _silu(y).astype(cat_ref.dtype)


_MIXER_PARAMS = ("win", "wout", "m_cw", "m_cb", "m_gbt", "m_nrm", "s_cw", "s_cb", "s_dtbt",
                 "s_alog", "s_alogt", "s_dskip", "s_nrm", "c_cw", "c_cb", "c_lg", "c_lb",
                 "tri", "trit", "expand")


P_PIECE = 256


def _project(hc, prm, p):
    for lo in range(0, P_COLS, P_PIECE):
        p[:, lo:lo + P_PIECE] = _dot(hc, prm["win"][:, lo:lo + P_PIECE])
        yield


def _round_robin(gens):
    gens = list(gens)
    while gens:
        for g in list(gens):
            try:
                next(g)
            except StopIteration:
                gens.remove(g)


def _mixer_kernel(x_ref, hb_ref, hn_ref, *refs):
    n_prm = len(_MIXER_PARAMS)
    prm = dict(zip(_MIXER_PARAMS, refs[:n_prm]))
    o_ref = refs[n_prm]
    extm_ref, exts_ref, extc_ref, c_ref, m_ref, h_ref, cat_ref, p_ref = refs[n_prm + 1:]
    tm = x_ref.shape[0]
    n_chunks = tm // CHUNK

    @pl.when((pl.program_id(0) == 0) & (pl.program_id(1) == 0))
    def _():
        for _ in _project(hb_ref[0:CHUNK, :], prm, p_ref.at[0]):
            pass

    @pl.when(pl.program_id(1) == 0)
    def _():
        extm_ref[0:CONV_CARRY, :] = jnp.zeros((CONV_CARRY, extm_ref.shape[1]), F32)
        exts_ref[0:CONV_CARRY, :] = jnp.zeros((CONV_CARRY, exts_ref.shape[1]), F32)
        extc_ref[0:C_CARRY, :] = jnp.zeros((C_CARRY, extc_ref.shape[1]), F32)
        c_ref[...] = jnp.zeros(c_ref.shape, F32)
        m_ref[...] = jnp.full(m_ref.shape, M_INIT_LOG, F32)
        h_ref[...] = jnp.zeros(h_ref.shape, F32)

    causal = _causal_mask()
    for c in range(n_chunks):
        rows = slice(c * CHUNK, (c + 1) * CHUNK)
        h_next = hb_ref[(c + 1) * CHUNK:(c + 2) * CHUNK, :] if c + 1 < n_chunks else hn_ref[...]
        p = p_ref.at[c % 2]
        proj = _project(h_next, prm, p_ref.at[(c + 1) % 2])
        _round_robin(
            [_mlstm_chunk(rows, p, prm, (extm_ref, c_ref, m_ref), cat_ref, causal), proj,
             _ssd_chunk(rows, p, prm, (exts_ref, h_ref), cat_ref, causal), proj,
             _convmod_chunk(rows, p, prm, extc_ref, cat_ref), proj])
        o_ref[rows, :] = x_ref[rows, :] + _dot(cat_ref[rows, :], prm["wout"][...])

    extm_ref[0:CONV_CARRY, :] = extm_ref[tm:tm + CONV_CARRY, :]
    exts_ref[0:CONV_CARRY, :] = exts_ref[tm:tm + CONV_CARRY, :]
    extc_ref[0:C_CARRY, :] = extc_ref[tm:tm + C_CARRY, :]


_MIXER_SHARED = ("tri", "trit", "expand")


def _mixer(x, hb, l, prm, batch, seq):
    d = x.shape[1]
    tm = TM_MIX
    nt = seq // tm
    params = [prm[k] for k in _MIXER_PARAMS]
    param_specs = [_resident(prm[k].shape) if k in _MIXER_SHARED else _layer(prm[k], l)
                   for k in _MIXER_PARAMS]
    n_chunks = tm // CHUNK
    assert n_chunks % 2 == 0 and seq % tm == 0
    last_chunk = batch * seq // CHUNK - 1
    next_chunk = lambda b, i: (jnp.minimum((b * nt + i + 1) * n_chunks, last_chunk), 0)
    return pl.pallas_call(
        _mixer_kernel,
        name="mixer",
        grid=(batch, nt),
        in_specs=[pl.BlockSpec((tm, d), lambda b, i: (b * nt + i, 0)),
                  pl.BlockSpec((tm, d), lambda b, i: (b * nt + i, 0)),
                  pl.BlockSpec((CHUNK, d), next_chunk)]
        + param_specs,
        out_specs=pl.BlockSpec((tm, d), lambda b, i: (b * nt + i, 0)),
        out_shape=jax.ShapeDtypeStruct((batch * seq, d), F32),
        scratch_shapes=[
            pltpu.VMEM((tm + CONV_CARRY, 2 * M_WIDTH), F32),
            pltpu.VMEM((tm + CONV_CARRY, XBC_WIDTH), F32),
            pltpu.VMEM((tm + C_CARRY, C_WIDTH), F32),
            pltpu.VMEM((M_HEADS, M_HEAD_DIM, 2 * M_HEAD_DIM), F32),
            pltpu.VMEM((M_HEADS, SUBLANES, LANES), F32),
            pltpu.VMEM((S_STATE, S_WIDTH), F32),
            pltpu.VMEM((tm, D_MIX), BF16),
            pltpu.VMEM((2, CHUNK, P_COLS), F32),
        ],
        compiler_params=pltpu.CompilerParams(
            dimension_semantics=("arbitrary", "arbitrary"), vmem_limit_bytes=VMEM_LIMIT),
    )(x, hb, hb, *params)


def _pad_lanes(a, width=LANES):
    return jnp.pad(a, [(0, 0)] * (a.ndim - 1) + [(0, width - a.shape[-1])])


def _rearrange_w_in(w):
    o_if = 4 * M_WIDTH
    o_z = o_if + 2 * M_HEADS
    o_xbc = o_z + S_WIDTH
    o_dt = o_xbc + XBC_WIDTH
    o_glu = o_dt + S_HEADS
    return jnp.concatenate([
        w[..., 0:o_if], w[..., o_xbc:o_dt], w[..., o_z:o_xbc], w[..., o_glu:],
        _pad_lanes(w[..., o_if:o_z]), _pad_lanes(w[..., o_dt:o_glu])], axis=-1).astype(BF16)


def kernel(x, ffn1_norm, ffn1_w_gate, ffn1_w_up, ffn1_w_down, mix_norm, w_in, w_out,
           mlstm_conv_w, mlstm_conv_b, mlstm_gate_b, mlstm_norm,
           ssd_conv_w, ssd_conv_b, ssd_dt_bias, ssd_a_log, ssd_d, ssd_norm,
           cm_conv_w, cm_conv_b, cm_ln_g, cm_ln_b,
           ffn2_norm, ffn2_w_gate, ffn2_w_up, ffn2_w_down, final_norm):
    batch, seq, d = x.shape
    depth = w_in.shape[0]
    xf = x.reshape(batch * seq, d)

    idx = jnp.arange(CHUNK)
    tri = (idx[None, :] <= idx[:, None]).astype(BF16)
    head_of_col = jnp.arange(S_WIDTH) // S_HEAD_DIM
    expand = (jnp.arange(LANES)[:, None] == head_of_col[None, :]).astype(BF16)
    rows = lambda a: a.reshape(depth, 1, -1).astype(F32)
    bf = lambda a: a.astype(BF16)
    lanes = lambda a: jnp.broadcast_to(a.astype(F32)[:, :, None], a.shape + (LANES,))
    fnorm = final_norm.reshape(1, -1).astype(F32)
    mix_g = rows(mix_norm)
    ffn1 = (rows(ffn1_norm), bf(ffn1_w_gate), bf(ffn1_w_up), bf(ffn1_w_down))
    ffn2 = (rows(ffn2_norm), bf(ffn2_w_gate), bf(ffn2_w_up), bf(ffn2_w_down))
    prm = dict(
        win=_rearrange_w_in(w_in), wout=bf(w_out),
        m_cw=mlstm_conv_w, m_cb=rows(mlstm_conv_b), m_gbt=lanes(mlstm_gate_b),
        m_nrm=rows(mlstm_norm),
        s_cw=ssd_conv_w, s_cb=rows(ssd_conv_b), s_dtbt=lanes(ssd_dt_bias),
        s_alog=_pad_lanes(rows(ssd_a_log)), s_alogt=lanes(ssd_a_log),
        s_dskip=rows(jnp.repeat(ssd_d, S_HEAD_DIM, axis=1)), s_nrm=rows(ssd_norm),
        c_cw=cm_conv_w, c_cb=rows(cm_conv_b), c_lg=rows(cm_ln_g), c_lb=rows(cm_ln_b),
        tri=tri, trit=tri.T, expand=expand)

    for l in range(depth):
        xf, hb = _ffn(xf, l, *ffn1, _layer(mix_g, l), mix_g, post="normed")
        xf = _mixer(xf, hb, l, prm, batch, seq)
        last = l == depth - 1
        xf = _ffn(xf, l, *ffn2, _resident(fnorm.shape), fnorm, post="final" if last else None)
    return xf.reshape(batch, seq, d)
```

```python
import functools

import jax
import jax.numpy as jnp
from jax import lax
from jax.experimental import pallas as pl
from jax.experimental.pallas import tpu as pltpu

F32 = jnp.float32
BF16 = jnp.bfloat16

RMS_EPS = 1e-6
LN_EPS = 1e-5
M_INIT_LOG = -1e30

CHUNK = 128
LANES = 128
SUBLANES = 8
M_HEADS = 4
M_HEAD_DIM = 128
M_WIDTH = M_HEADS * M_HEAD_DIM
M_CONV = 4
S_HEADS = 16
S_HEAD_DIM = 64
S_GROUPS = 4
S_STATE = 128
S_WIDTH = S_HEADS * S_HEAD_DIM
S_GROUP_WIDTH = S_WIDTH // S_GROUPS
S_HEADS_PER_GROUP = S_HEADS // S_GROUPS
S_CONV = 4
XBC_WIDTH = S_WIDTH + 2 * S_GROUPS * S_STATE
C_WIDTH = 512
C_KERNEL = 31
CONV_CARRY = SUBLANES
C_CARRY = 32
D_MIX = M_WIDTH + S_WIDTH + C_WIDTH

P_QKVO = 0
P_XBC = 2048
P_Z = 4096
P_GLU = 5120
P_GATES = 6144
P_COLS = 6400

TM_DENSE = 512
TM_MIX = 256
VMEM_LIMIT = 56 * 1024 * 1024


def _sigmoid(x):
    return 1.0 / (1.0 + jnp.exp(-x))


def _silu(x):
    return x * _sigmoid(x)


def _softplus(x):
    return jnp.maximum(x, 0.0) + jnp.log1p(jnp.exp(-jnp.abs(x)))


def _log_sigmoid(x):
    return jnp.minimum(x, 0.0) - jnp.log1p(jnp.exp(-jnp.abs(x)))


def _rms(x, g):
    return x * lax.rsqrt(jnp.mean(x * x, axis=-1, keepdims=True) + RMS_EPS) * g


def _dot(a, b):
    return jnp.dot(a, b, preferred_element_type=F32)


def _dot_nt(a, b):
    return lax.dot_general(a, b, (((1,), (1,)), ((), ())), preferred_element_type=F32)


def _dot_tn(a, b):
    return lax.dot_general(a, b, (((0,), (0,)), ((), ())), preferred_element_type=F32)


def _split3(a):
    hi = a.astype(BF16)
    r = a - hi.astype(F32)
    mid = r.astype(BF16)
    lo = (r - mid.astype(F32)).astype(BF16)
    return hi, mid, lo


def _dot_split_rhs(a_bf16, b_f32):
    hi, mid, lo = _split3(b_f32)
    return _dot(a_bf16, hi) + _dot(a_bf16, mid) + _dot(a_bf16, lo)


def _dot_split_lhs(a_f32, b_bf16):
    hi, mid, lo = _split3(a_f32)
    return _dot(hi, b_bf16) + _dot(mid, b_bf16) + _dot(lo, b_bf16)


def _expand(v, e):
    hi = v.astype(BF16)
    lo = (v - hi.astype(F32)).astype(BF16)
    return _dot(hi, e) + _dot(lo, e)


def _causal_mask():
    t = lax.broadcasted_iota(jnp.int32, (CHUNK, CHUNK), 0)
    s = lax.broadcasted_iota(jnp.int32, (CHUNK, CHUNK), 1)
    return s <= t


def _causal_conv(ext_ref, r0, carry, width, w_ref, b_ref, cols):
    win = ext_ref[r0:r0 + carry + CHUNK, cols]
    acc = b_ref[:, cols]
    for k in range(SUBLANES):
        taps = [j for j in range(width) if (width - 1 - j) % SUBLANES == k]
        if not taps:
            continue
        rolled = win if k == 0 else pltpu.roll(win, k, 0)
        for j in taps:
            back = width - 1 - j
            start = carry - (back - k)
            acc = acc + w_ref[j:j + 1, cols] * rolled[start:start + CHUNK, :]
    return acc


def _ffn_kernel(x_ref, g_ref, wg_ref, wu_ref, wd_ref, ng_ref, o_ref, *maybe_h_ref, post):
    x = x_ref[...]
    h = _rms(x, g_ref[...]).astype(BF16)
    a = _dot(h, wg_ref[...])
    b = _dot(h, wu_ref[...])
    t = (_silu(a) * b).astype(BF16)
    y = x + 0.5 * _dot(t, wd_ref[...])
    if post == "final":
        y = _rms(y, ng_ref[...])
    o_ref[...] = y
    if post == "normed":
        maybe_h_ref[0][...] = _rms(y, ng_ref[...]).astype(BF16)


def _resident(shape):
    return pl.BlockSpec(shape, lambda *_: (0,) * len(shape), pipeline_mode=pl.Buffered(1))


def _layer(a, l):
    return pl.BlockSpec((None,) + a.shape[1:], lambda *_: (l,) + (0,) * (a.ndim - 1),
                        pipeline_mode=pl.Buffered(1))


def _ffn(x, l, g, wg, wu, wd, ng_spec, ng, post):
    t, d = x.shape
    tm = TM_DENSE
    tile = pl.BlockSpec((tm, d), lambda i: (i, 0))
    out_specs, out_shape = tile, jax.ShapeDtypeStruct((t, d), F32)
    if post == "normed":
        out_specs, out_shape = (tile, tile), (out_shape, jax.ShapeDtypeStruct((t, d), BF16))
    return pl.pallas_call(
        functools.partial(_ffn_kernel, post=post),
        name="ffn_" + str(post).lower(),
        grid=(t // tm,),
        in_specs=[tile, _layer(g, l), _layer(wg, l), _layer(wu, l), _layer(wd, l), ng_spec],
        out_specs=out_specs,
        out_shape=out_shape,
        compiler_params=pltpu.CompilerParams(
            dimension_semantics=("arbitrary",), vmem_limit_bytes=VMEM_LIMIT),
    )(x, g, wg, wu, wd, ng)


def _mlstm_chunk(rows, p, prm, scr, cat_ref, causal):
    extm_ref, c_ref, m_ref = scr
    r0 = rows.start
    extm_ref[CONV_CARRY + r0:CONV_CARRY + r0 + CHUNK, :] = p[:, P_QKVO:P_QKVO + 2 * M_WIDTH]
    conv = lambda cols: _silu(_causal_conv(extm_ref, r0, CONV_CARRY, M_CONV, prm["m_cw"], prm["m_cb"], cols))
    q_all = conv(slice(0, M_WIDTH))
    yield
    k_all = conv(slice(M_WIDTH, 2 * M_WIDTH)) * (M_HEAD_DIM ** -0.5)
    yield

    gg = p[:, P_GATES:P_GATES + LANES].T[0:2 * M_HEADS, :] + prm["m_gbt"][...]
    row = lax.broadcasted_iota(jnp.int32, gg.shape, 0)
    gt = jnp.where(row >= M_HEADS, _log_sigmoid(gg), gg)
    gc = jnp.concatenate([gt, jnp.zeros((LANES - 2 * M_HEADS, CHUNK), F32)], axis=0).T
    b_cols = _dot_split_rhs(prm["tri"][...], gc)
    b_rows = _dot_split_lhs(gt, prm["trit"][...])
    ones = jnp.ones((CHUNK, M_HEAD_DIM), BF16)
    yield

    for h in range(M_HEADS):
        cols = slice(h * M_HEAD_DIM, (h + 1) * M_HEAD_DIM)
        q_h = q_all[:, cols]
        k_h = k_all[:, cols]
        v_off = P_QKVO + 2 * M_WIDTH + h * M_HEAD_DIM
        o_off = P_QKVO + 3 * M_WIDTH + h * M_HEAD_DIM
        v_h = p[:, v_off:v_off + M_HEAD_DIM]
        o_h = p[:, o_off:o_off + M_HEAD_DIM]
        b_col = b_cols[:, M_HEADS + h:M_HEADS + h + 1]
        i_col = gc[:, h:h + 1]
        b_row = b_rows[M_HEADS + h:M_HEADS + h + 1, :]
        i_row = gt[h:h + 1, :]
        m = m_ref[h, 0:1, 0:1]
        g = b_col[CHUNK - 1:CHUNK, :]

        dmat = jnp.where(causal, (b_col - b_row) + i_row, -jnp.inf)
        inter = b_col + m
        m_t = jnp.maximum(inter, jnp.max(dmat, axis=1, keepdims=True))
        w = jnp.exp(dmat - m_t)
        sqk = (_dot_nt(q_h.astype(BF16), k_h.astype(BF16)) * w).astype(BF16)
        a = jnp.exp(inter - m_t)
        v_aug = jnp.concatenate([v_h.astype(BF16), ones], axis=1)
        c_aug = c_ref[h]
        lhs = jnp.concatenate([sqk, (a * q_h).astype(BF16)], axis=1)
        res = _dot(lhs, jnp.concatenate([v_aug, c_aug.astype(BF16)], axis=0))
        num = res[:, :M_HEAD_DIM]
        den = res[:, M_HEAD_DIM:M_HEAD_DIM + 1]
        hh = num / jnp.maximum(jnp.abs(den), jnp.exp(-m_t))

        u = (g - b_col) + i_col
        m_new = jnp.maximum(g + m, jnp.max(u, axis=0, keepdims=True))
        ws = jnp.exp(u - m_new)
        decay = jnp.exp(g + m - m_new)
        kw = (k_h * ws).astype(BF16)
        c_ref[h] = decay * c_aug + _dot_tn(kw, v_aug)
        m_ref[h] = jnp.broadcast_to(m_new, m_ref.shape[1:])

        out = _sigmoid(o_h) * hh
        cat_ref[rows, cols] = _rms(out, prm["m_nrm"][:, cols]).astype(cat_ref.dtype)
        yield


def _ssd_chunk(rows, p, prm, scr, cat_ref, causal):
    exts_ref, h_ref = scr
    r0 = rows.start
    exts_ref[CONV_CARRY + r0:CONV_CARRY + r0 + CHUNK, :] = p[:, P_XBC:P_XBC + XBC_WIDTH]
    conv = lambda off, n: _silu(_causal_conv(exts_ref, r0, CONV_CARRY, S_CONV, prm["s_cw"], prm["s_cb"],
                                             slice(off, off + n)))

    dt_t = _softplus(p[:, P_GATES + LANES:P_COLS].T[0:S_HEADS, :] + prm["s_dtbt"][...])
    adt_t = dt_t * (-jnp.exp(prm["s_alogt"][...]))
    dtc = jnp.concatenate([dt_t, jnp.zeros((LANES - S_HEADS, CHUNK), F32)], axis=0).T
    adt = dtc * (-jnp.exp(prm["s_alog"][...]))
    e = prm["expand"][...]
    cs_col = _dot_split_rhs(prm["tri"][...], adt)
    cs_row = _dot_split_lhs(adt_t, prm["trit"][...])
    cs_last = cs_col[CHUNK - 1:CHUNK, :]
    ex = _dot(jnp.exp(cs_col).astype(BF16), e)
    dx = _dot((jnp.exp(cs_last - cs_col) * dtc).astype(BF16), e)
    elx = _expand(jnp.broadcast_to(jnp.exp(cs_last), (SUBLANES, LANES)), e)[0:1, :]
    seg = lax.broadcasted_iota(jnp.int32, (CHUNK, S_GROUP_WIDTH), 1) // S_HEAD_DIM
    yield

    for g in range(S_GROUPS):
        gcols = slice(g * S_GROUP_WIDTH, (g + 1) * S_GROUP_WIDTH)
        xs = conv(g * S_GROUP_WIDTH, S_GROUP_WIDTH)
        xs_b = xs.astype(BF16)
        b_g = conv(S_WIDTH + g * S_STATE, S_STATE).astype(BF16)
        c_g = conv(S_WIDTH + S_GROUPS * S_STATE + g * S_STATE, S_STATE).astype(BF16)
        yield
        cbm = _dot_nt(c_g, b_g)
        h_g = h_ref[:, gcols]
        y_off = _dot(c_g, h_g.astype(BF16)) * ex[:, gcols]
        ms, xst = [], []
        for r in range(S_HEADS_PER_GROUP):
            j = g * S_HEADS_PER_GROUP + r
            lmat = jnp.exp(jnp.where(causal, cs_col[:, j:j + 1] - cs_row[j:j + 1, :], -jnp.inf))
            ms.append((cbm * lmat * dt_t[j:j + 1, :]).astype(BF16))
            xst.append(jnp.where(seg == r, xs_b, jnp.zeros_like(xs_b)))
        y_diag = _dot(jnp.concatenate(ms, axis=1), jnp.concatenate(xst, axis=0))
        xd = (xs * dx[:, gcols]).astype(BF16)
        h_ref[:, gcols] = elx[:, gcols] * h_g + _dot_tn(b_g, xd)

        y = y_diag + y_off + prm["s_dskip"][:, gcols] * xs
        y = y * _silu(p[:, P_Z + g * S_GROUP_WIDTH:P_Z + (g + 1) * S_GROUP_WIDTH])
        ocols = slice(M_WIDTH + g * S_GROUP_WIDTH, M_WIDTH + (g + 1) * S_GROUP_WIDTH)
        cat_ref[rows, ocols] = _rms(y, prm["s_nrm"][:, gcols]).astype(cat_ref.dtype)
        yield


def _convmod_chunk(rows, p, prm, extc_ref, cat_ref):
    r0 = rows.start
    gate = _sigmoid(p[:, P_GLU + C_WIDTH:P_GLU + 2 * C_WIDTH])
    extc_ref[C_CARRY + r0:C_CARRY + r0 + CHUNK, :] = p[:, P_GLU:P_GLU + C_WIDTH] * gate
    yield
    pieces = []
    for lo in range(0, C_WIDTH, LANES):
        pieces.append(_causal_conv(extc_ref, r0, C_CARRY, C_KERNEL, prm["c_cw"], prm["c_cb"],
                                   slice(lo, lo + LANES)))
        yield
    acc = jnp.concatenate(pieces, axis=1)
    mu = jnp.mean(acc, axis=-1, keepdims=True)
    xc = acc - mu
    y = xc * lax.rsqrt(jnp.mean(xc * xc, axis=-1, keepdims=True) + LN_EPS)
    y = y * prm["c_lg"][...] + prm["c_lb"][...]
    cat_ref[rows, M_WIDTH + S_WIDTH:] = _silu(y).astype(cat_ref.dtype)


_MIXER_PARAMS = ("win", "wout", "m_cw", "m_cb", "m_gbt", "m_nrm", "s_cw", "s_cb", "s_dtbt",
                 "s_alog", "s_alogt", "s_dskip", "s_nrm", "c_cw", "c_cb", "c_lg", "c_lb",
                 "tri", "trit", "expand")


P_PIECE = 256


def _project(hc, prm, p):
    for lo in range(0, P_COLS, P_PIECE):
        p[:, lo:lo + P_PIECE] = _dot(hc, prm["win"][:, lo:lo + P_PIECE])
        yield


def _round_robin(gens):
    gens = list(gens)
    while gens:
        for g in list(gens):
            try:
                next(g)
            except StopIteration:
                gens.remove(g)


def _mixer_kernel(x_ref, hb_ref, hn_ref, *refs):
    n_prm = len(_MIXER_PARAMS)
    prm = dict(zip(_MIXER_PARAMS, refs[:n_prm]))
    o_ref = refs[n_prm]
    extm_ref, exts_ref, extc_ref, c_ref, m_ref, h_ref, cat_ref, p_ref = refs[n_prm + 1:]
    tm = x_ref.shape[0]
    n_chunks = tm // CHUNK

    @pl.when((pl.program_id(0) == 0) & (pl.program_id(1) == 0))
    def _():
        for _ in _project(hb_ref[0:CHUNK, :], prm, p_ref.at[0]):
            pass

    @pl.when(pl.program_id(1) == 0)
    def _():
        extm_ref[0:CONV_CARRY, :] = jnp.zeros((CONV_CARRY, extm_ref.shape[1]), F32)
        exts_ref[0:CONV_CARRY, :] = jnp.zeros((CONV_CARRY, exts_ref.shape[1]), F32)
        extc_ref[0:C_CARRY, :] = jnp.zeros((C_CARRY, extc_ref.shape[1]), F32)
        c_ref[...] = jnp.zeros(c_ref.shape, F32)
        m_ref[...] = jnp.full(m_ref.shape, M_INIT_LOG, F32)
        h_ref[...] = jnp.zeros(h_ref.shape, F32)

    causal = _causal_mask()
    for c in range(n_chunks):
        rows = slice(c * CHUNK, (c + 1) * CHUNK)
        h_next = hb_ref[(c + 1) * CHUNK:(c + 2) * CHUNK, :] if c + 1 < n_chunks else hn_ref[...]
        p = p_ref.at[c % 2]
        proj = _project(h_next, prm, p_ref.at[(c + 1) % 2])
        _round_robin(
            [_mlstm_chunk(rows, p, prm, (extm_ref, c_ref, m_ref), cat_ref, causal), proj,
             _ssd_chunk(rows, p, prm, (exts_ref, h_ref), cat_ref, causal), proj,
             _convmod_chunk(rows, p, prm, extc_ref, cat_ref), proj])
        o_ref[rows, :] = x_ref[rows, :] + _dot(cat_ref[rows, :], prm["wout"][...])

    extm_ref[0:CONV_CARRY, :] = extm_ref[tm:tm + CONV_CARRY, :]
    exts_ref[0:CONV_CARRY, :] = exts_ref[tm:tm + CONV_CARRY, :]
    extc_ref[0:C_CARRY, :] = extc_ref[tm:tm + C_CARRY, :]


_MIXER_SHARED = ("tri", "trit", "expand")


def _mixer(x, hb, l, prm, batch, seq):
    d = x.shape[1]
    tm = TM_MIX
    nt = seq // tm
    params = [prm[k] for k in _MIXER_PARAMS]
    param_specs = [_resident(prm[k].shape) if k in _MIXER_SHARED else _layer(prm[k], l)
                   for k in _MIXER_PARAMS]
    n_chunks = tm // CHUNK
    assert n_chunks % 2 == 0 and seq % tm == 0
    last_chunk = batch * seq // CHUNK - 1
    next_chunk = lambda b, i: (jnp.minimum((b * nt + i + 1) * n_chunks, last_chunk), 0)
    return pl.pallas_call(
        _mixer_kernel,
        name="mixer",
        grid=(batch, nt),
        in_specs=[pl.BlockSpec((tm, d), lambda b, i: (b * nt + i, 0)),
                  pl.BlockSpec((tm, d), lambda b, i: (b * nt + i, 0)),
                  pl.BlockSpec((CHUNK, d), next_chunk)]
        + param_specs,
        out_specs=pl.BlockSpec((tm, d), lambda b, i: (b * nt + i, 0)),
        out_shape=jax.ShapeDtypeStruct((batch * seq, d), F32),
        scratch_shapes=[
            pltpu.VMEM((tm + CONV_CARRY, 2 * M_WIDTH), F32),
            pltpu.VMEM((tm + CONV_CARRY, XBC_WIDTH), F32),
            pltpu.VMEM((tm + C_CARRY, C_WIDTH), F32),
            pltpu.VMEM((M_HEADS, M_HEAD_DIM, 2 * M_HEAD_DIM), F32),
            pltpu.VMEM((M_HEADS, SUBLANES, LANES), F32),
            pltpu.VMEM((S_STATE, S_WIDTH), F32),
            pltpu.VMEM((tm, D_MIX), BF16),
            pltpu.VMEM((2, CHUNK, P_COLS), F32),
        ],
        compiler_params=pltpu.CompilerParams(
            dimension_semantics=("arbitrary", "arbitrary"), vmem_limit_bytes=VMEM_LIMIT),
    )(x, hb, hb, *params)


def _pad_lanes(a, width=LANES):
    return jnp.pad(a, [(0, 0)] * (a.ndim - 1) + [(0, width - a.shape[-1])])


def _rearrange_w_in(w):
    o_if = 4 * M_WIDTH
    o_z = o_if + 2 * M_HEADS
    o_xbc = o_z + S_WIDTH
    o_dt = o_xbc + XBC_WIDTH
    o_glu = o_dt + S_HEADS
    return jnp.concatenate([
        w[..., 0:o_if], w[..., o_xbc:o_dt], w[..., o_z:o_xbc], w[..., o_glu:],
        _pad_lanes(w[..., o_if:o_z]), _pad_lanes(w[..., o_dt:o_glu])], axis=-1).astype(BF16)


def kernel(x, ffn1_norm, ffn1_w_gate, ffn1_w_up, ffn1_w_down, mix_norm, w_in, w_out,
           mlstm_conv_w, mlstm_conv_b, mlstm_gate_b, mlstm_norm,
           ssd_conv_w, ssd_conv_b, ssd_dt_bias, ssd_a_log, ssd_d, ssd_norm,
           cm_conv_w, cm_conv_b, cm_ln_g, cm_ln_b,
           ffn2_norm, ffn2_w_gate, ffn2_w_up, ffn2_w_down, final_norm):
    batch, seq, d = x.shape
    depth = w_in.shape[0]
    xf = x.reshape(batch * seq, d)

    idx = jnp.arange(CHUNK)
    tri = (idx[None, :] <= idx[:, None]).astype(BF16)
    head_of_col = jnp.arange(S_WIDTH) // S_HEAD_DIM
    expand = (jnp.arange(LANES)[:, None] == head_of_col[None, :]).astype(BF16)
    rows = lambda a: a.reshape(depth, 1, -1).astype(F32)
    bf = lambda a: a.astype(BF16)
    lanes = lambda a: jnp.broadcast_to(a.astype(F32)[:, :, None], a.shape + (LANES,))
    fnorm = final_norm.reshape(1, -1).astype(F32)
    mix_g = rows(mix_norm)
    ffn1 = (rows(ffn1_norm), bf(ffn1_w_gate), bf(ffn1_w_up), bf(ffn1_w_down))
    ffn2 = (rows(ffn2_norm), bf(ffn2_w_gate), bf(ffn2_w_up), bf(ffn2_w_down))
    prm = dict(
        win=_rearrange_w_in(w_in), wout=bf(w_out),
        m_cw=mlstm_conv_w, m_cb=rows(mlstm_conv_b), m_gbt=lanes(mlstm_gate_b),
        m_nrm=rows(mlstm_norm),
        s_cw=ssd_conv_w, s_cb=rows(ssd_conv_b), s_dtbt=lanes(ssd_dt_bias),
        s_alog=_pad_lanes(rows(ssd_a_log)), s_alogt=lanes(ssd_a_log),
        s_dskip=rows(jnp.repeat(ssd_d, S_HEAD_DIM, axis=1)), s_nrm=rows(ssd_norm),
        c_cw=cm_conv_w, c_cb=rows(cm_conv_b), c_lg=rows(cm_ln_g), c_lb=rows(cm_ln_b),
        tri=tri, trit=tri.T, expand=expand)

    for l in range(depth):
        xf, hb = _ffn(xf, l, *ffn1, _layer(mix_g, l), mix_g, post="normed")
        xf = _mixer(xf, hb, l, prm, batch, seq)
        last = l == depth - 1
        xf = _ffn(xf, l, *ffn2, _resident(fnorm.shape), fnorm, post="final" if last else None)
    return xf.reshape(batch, seq, d)
```

```python
import functools

import jax
import jax.numpy as jnp
from jax import lax
from jax.experimental import pallas as pl
from jax.experimental.pallas import tpu as pltpu

F32 = jnp.float32
BF16 = jnp.bfloat16

RMS_EPS = 1e-6
LN_EPS = 1e-5
M_INIT_LOG = -1e30

CHUNK = 128
LANES = 128
SUBLANES = 8
M_HEADS = 4
M_HEAD_DIM = 128
M_WIDTH = M_HEADS * M_HEAD_DIM
M_CONV = 4
S_HEADS = 16
S_HEAD_DIM = 64
S_GROUPS = 4
S_STATE = 128
S_WIDTH = S_HEADS * S_HEAD_DIM
S_GROUP_WIDTH = S_WIDTH // S_GROUPS
S_HEADS_PER_GROUP = S_HEADS // S_GROUPS
S_CONV = 4
XBC_WIDTH = S_WIDTH + 2 * S_GROUPS * S_STATE
C_WIDTH = 512
C_KERNEL = 31
CONV_CARRY = SUBLANES
C_CARRY = 32
D_MIX = M_WIDTH + S_WIDTH + C_WIDTH

P_QKVO = 0
P_XBC = 2048
P_Z = 4096
P_GLU = 5120
P_GATES = 6144
P_COLS = 6400

TM_DENSE = 512
TM_MIX = 512
VMEM_LIMIT = 56 * 1024 * 1024


def _sigmoid(x):
    return 1.0 / (1.0 + jnp.exp(-x))


def _silu(x):
    return x * _sigmoid(x)


def _softplus(x):
    return jnp.maximum(x, 0.0) + jnp.log1p(jnp.exp(-jnp.abs(x)))


def _log_sigmoid(x):
    return jnp.minimum(x, 0.0) - jnp.log1p(jnp.exp(-jnp.abs(x)))


def _rms(x, g):
    return x * lax.rsqrt(jnp.mean(x * x, axis=-1, keepdims=True) + RMS_EPS) * g


def _dot(a, b):
    return jnp.dot(a, b, preferred_element_type=F32)


def _dot_nt(a, b):
    return lax.dot_general(a, b, (((1,), (1,)), ((), ())), preferred_element_type=F32)


def _dot_tn(a, b):
    return lax.dot_general(a, b, (((0,), (0,)), ((), ())), preferred_element_type=F32)


def _split3(a):
    hi = a.astype(BF16)
    r = a - hi.astype(F32)
    mid = r.astype(BF16)
    lo = (r - mid.astype(F32)).astype(BF16)
    return hi, mid, lo


def _dot_split_rhs(a_bf16, b_f32):
    hi, mid, lo = _split3(b_f32)
    return _dot(a_bf16, hi) + _dot(a_bf16, mid) + _dot(a_bf16, lo)


def _dot_split_lhs(a_f32, b_bf16):
    hi, mid, lo = _split3(a_f32)
    return _dot(hi, b_bf16) + _dot(mid, b_bf16) + _dot(lo, b_bf16)


def _expand(v, e):
    hi = v.astype(BF16)
    lo = (v - hi.astype(F32)).astype(BF16)
    return _dot(hi, e) + _dot(lo, e)


def _causal_mask():
    t = lax.broadcasted_iota(jnp.int32, (CHUNK, CHUNK), 0)
    s = lax.broadcasted_iota(jnp.int32, (CHUNK, CHUNK), 1)
    return s <= t


def _causal_conv(ext_ref, r0, carry, width, w_ref, b_ref, cols):
    win = ext_ref[r0:r0 + carry + CHUNK, cols]
    acc = b_ref[:, cols]
    if width == 4:
        w = [w_ref[j:j + 1, cols] for j in range(width)]
        win2 = pltpu.roll(win, 2, 0)
        even = w[3] * win[carry:, :] + w[1] * win2[carry:, :]
        odd = pltpu.roll(w[2] * win + w[0] * win2, 1, 0)[carry:, :]
        return (acc + even) + odd
    for k in range(SUBLANES):
        taps = [j for j in range(width) if (width - 1 - j) % SUBLANES == k]
        if not taps:
            continue
        rolled = win if k == 0 else pltpu.roll(win, k, 0)
        for j in taps:
            back = width - 1 - j
            start = carry - (back - k)
            acc = acc + w_ref[j:j + 1, cols] * rolled[start:start + CHUNK, :]
    return acc


def _ffn_kernel(x_ref, g_ref, wg_ref, wu_ref, wd_ref, ng_ref, o_ref, *maybe_h_ref, post):
    x = x_ref[...]
    h = _rms(x, g_ref[...]).astype(BF16)
    a = _dot(h, wg_ref[...])
    b = _dot(h, wu_ref[...])
    t = (_silu(a) * b).astype(BF16)
    y = x + 0.5 * _dot(t, wd_ref[...])
    if post == "final":
        y = _rms(y, ng_ref[...])
    o_ref[...] = y
    if post == "normed":
        maybe_h_ref[0][...] = _rms(y, ng_ref[...]).astype(BF16)


def _resident(shape):
    return pl.BlockSpec(shape, lambda *_: (0,) * len(shape), pipeline_mode=pl.Buffered(1))


def _layer(a, l):
    return pl.BlockSpec((None,) + a.shape[1:], lambda *_: (l,) + (0,) * (a.ndim - 1),
                        pipeline_mode=pl.Buffered(1))


def _ffn(x, l, g, wg, wu, wd, ng_spec, ng, post):
    t, d = x.shape
    tm = TM_DENSE
    tile = pl.BlockSpec((tm, d), lambda i: (i, 0))
    out_specs, out_shape = tile, jax.ShapeDtypeStruct((t, d), F32)
    if post == "normed":
        out_specs, out_shape = (tile, tile), (out_shape, jax.ShapeDtypeStruct((t, d), BF16))
    return pl.pallas_call(
        functools.partial(_ffn_kernel, post=post),
        name="ffn_" + str(post).lower(),
        grid=(t // tm,),
        in_specs=[tile, _layer(g, l), _layer(wg, l), _layer(wu, l), _layer(wd, l), ng_spec],
        out_specs=out_specs,
        out_shape=out_shape,
        compiler_params=pltpu.CompilerParams(
            dimension_semantics=("arbitrary",), vmem_limit_bytes=VMEM_LIMIT),
    )(x, g, wg, wu, wd, ng)


def _mlstm_chunk(rows, p, prm, scr, cat_ref, causal):
    extm_ref, c_ref, m_ref = scr
    r0 = rows.start
    extm_ref[CONV_CARRY + r0:CONV_CARRY + r0 + CHUNK, :] = p[:, P_QKVO:P_QKVO + 2 * M_WIDTH]
    conv = lambda cols: _silu(_causal_conv(extm_ref, r0, CONV_CARRY, M_CONV, prm["m_cw"], prm["m_cb"], cols))
    q_all = conv(slice(0, M_WIDTH))
    yield
    k_all = conv(slice(M_WIDTH, 2 * M_WIDTH)) * (M_HEAD_DIM ** -0.5)
    yield

    gg = p[:, P_GATES:P_GATES + LANES].T[0:2 * M_HEADS, :] + prm["m_gbt"][...]
    row = lax.broadcasted_iota(jnp.int32, gg.shape, 0)
    gt = jnp.where(row >= M_HEADS, _log_sigmoid(gg), gg)
    gc = jnp.concatenate([gt, jnp.zeros((LANES - 2 * M_HEADS, CHUNK), F32)], axis=0).T
    b_cols = _dot_split_rhs(prm["tri"][...], gc)
    b_rows = _dot_split_lhs(gt, prm["trit"][...])
    ones = jnp.ones((CHUNK, M_HEAD_DIM), BF16)
    yield

    for h in range(M_HEADS):
        cols = slice(h * M_HEAD_DIM, (h + 1) * M_HEAD_DIM)
        q_h = q_all[:, cols]
        k_h = k_all[:, cols]
        v_off = P_QKVO + 2 * M_WIDTH + h * M_HEAD_DIM
        o_off = P_QKVO + 3 * M_WIDTH + h * M_HEAD_DIM
        v_h = p[:, v_off:v_off + M_HEAD_DIM]
        o_h = p[:, o_off:o_off + M_HEAD_DIM]
        b_col = b_cols[:, M_HEADS + h:M_HEADS + h + 1]
        i_col = gc[:, h:h + 1]
        b_row = b_rows[M_HEADS + h:M_HEADS + h + 1, :]
        i_row = gt[h:h + 1, :]
        m = m_ref[h, 0:1, 0:1]
        g = b_col[CHUNK - 1:CHUNK, :]

        dmat = jnp.where(causal, (b_col - b_row) + i_row, -jnp.inf)
        inter = b_col + m
        m_t = jnp.maximum(inter, jnp.max(dmat, axis=1, keepdims=True))
        w = jnp.exp(dmat - m_t)
        sqk = (_dot_nt(q_h.astype(BF16), k_h.astype(BF16)) * w).astype(BF16)
        a = jnp.exp(inter - m_t)
        v_aug = jnp.concatenate([v_h.astype(BF16), ones], axis=1)
        c_aug = c_ref[h]
        lhs = jnp.concatenate([sqk, (a * q_h).astype(BF16)], axis=1)
        res = _dot(lhs, jnp.concatenate([v_aug, c_aug.astype(BF16)], axis=0))
        num = res[:, :M_HEAD_DIM]
        den = res[:, M_HEAD_DIM:M_HEAD_DIM + 1]
        hh = num / jnp.maximum(jnp.abs(den), jnp.exp(-m_t))

        u = (g - b_col) + i_col
        m_new = jnp.maximum(g + m, jnp.max(u, axis=0, keepdims=True))
        ws = jnp.exp(u - m_new)
        decay = jnp.exp(g + m - m_new)
        kw = (k_h * ws).astype(BF16)
        c_ref[h] = decay * c_aug + _dot_tn(kw, v_aug)
        m_ref[h] = jnp.broadcast_to(m_new, m_ref.shape[1:])

        out = _sigmoid(o_h) * hh
        cat_ref[rows, cols] = _rms(out, prm["m_nrm"][:, cols]).astype(cat_ref.dtype)
        yield


def _ssd_chunk(rows, p, prm, scr, cat_ref, causal):
    exts_ref, h_ref = scr
    r0 = rows.start
    exts_ref[CONV_CARRY + r0:CONV_CARRY + r0 + CHUNK, :] = p[:, P_XBC:P_XBC + XBC_WIDTH]
    conv = lambda off, n: _silu(_causal_conv(exts_ref, r0, CONV_CARRY, S_CONV, prm["s_cw"], prm["s_cb"],
                                             slice(off, off + n)))

    dt_t = _softplus(p[:, P_GATES + LANES:P_COLS].T[0:S_HEADS, :] + prm["s_dtbt"][...])
    adt_t = dt_t * (-jnp.exp(prm["s_alogt"][...]))
    dtc = jnp.concatenate([dt_t, jnp.zeros((LANES - S_HEADS, CHUNK), F32)], axis=0).T
    adt = dtc * (-jnp.exp(prm["s_alog"][...]))
    e = prm["expand"][...]
    cs_col = _dot_split_rhs(prm["tri"][...], adt)
    cs_row = _dot_split_lhs(adt_t, prm["trit"][...])
    cs_last = cs_col[CHUNK - 1:CHUNK, :]
    ex = _dot(jnp.exp(cs_col).astype(BF16), e)
    dx = _dot((jnp.exp(cs_last - cs_col) * dtc).astype(BF16), e)
    elx = _expand(jnp.broadcast_to(jnp.exp(cs_last), (SUBLANES, LANES)), e)[0:1, :]
    seg = lax.broadcasted_iota(jnp.int32, (CHUNK, S_GROUP_WIDTH), 1) // S_HEAD_DIM
    yield

    for g in range(S_GROUPS):
        gcols = slice(g * S_GROUP_WIDTH, (g + 1) * S_GROUP_WIDTH)
        xs = conv(g * S_GROUP_WIDTH, S_GROUP_WIDTH)
        xs_b = xs.astype(BF16)
        b_g = conv(S_WIDTH + g * S_STATE, S_STATE).astype(BF16)
        c_g = conv(S_WIDTH + S_GROUPS * S_STATE + g * S_STATE, S_STATE).astype(BF16)
        yield
        cbm = _dot_nt(c_g, b_g)
        h_g = h_ref[:, gcols]
        y_off = _dot(c_g, h_g.astype(BF16)) * ex[:, gcols]
        ms, xst = [], []
        for r in range(S_HEADS_PER_GROUP):
            j = g * S_HEADS_PER_GROUP + r
            lmat = jnp.exp(jnp.where(causal, cs_col[:, j:j + 1] - cs_row[j:j + 1, :], -jnp.inf))
            ms.append((cbm * lmat * dt_t[j:j + 1, :]).astype(BF16))
            xst.append(jnp.where(seg == r, xs_b, jnp.zeros_like(xs_b)))
        y_diag = _dot(jnp.concatenate(ms, axis=1), jnp.concatenate(xst, axis=0))
        xd = (xs * dx[:, gcols]).astype(BF16)
        h_ref[:, gcols] = elx[:, gcols] * h_g + _dot_tn(b_g, xd)

        y = y_diag + y_off + prm["s_dskip"][:, gcols] * xs
        y = y * _silu(p[:, P_Z + g * S_GROUP_WIDTH:P_Z + (g + 1) * S_GROUP_WIDTH])
        ocols = slice(M_WIDTH + g * S_GROUP_WIDTH, M_WIDTH + (g + 1) * S_GROUP_WIDTH)
        cat_ref[rows, ocols] = _rms(y, prm["s_nrm"][:, gcols]).astype(cat_ref.dtype)
        yield


def _convmod_chunk(rows, p, prm, extc_ref, cat_ref):
    r0 = rows.start
    gate = _sigmoid(p[:, P_GLU + C_WIDTH:P_GLU + 2 * C_WIDTH])
    extc_ref[C_CARRY + r0:C_CARRY + r0 + CHUNK, :] = p[:, P_GLU:P_GLU + C_WIDTH] * gate
    yield
    pieces = []
    for lo in range(0, C_WIDTH, LANES):
        pieces.append(_causal_conv(extc_ref, r0, C_CARRY, C_KERNEL, prm["c_cw"], prm["c_cb"],
                                   slice(lo, lo + LANES)))
        yield
    acc = jnp.concatenate(pieces, axis=1)
    mu = jnp.mean(acc, axis=-1, keepdims=True)
    xc = acc - mu
    y = xc * lax.rsqrt(jnp.mean(xc * xc, axis=-1, keepdims=True) + LN_EPS)
    y = y * prm["c_lg"][...] + prm["c_lb"][...]
    cat_ref[rows, M_WIDTH + S_WIDTH:] = _silu(y).astype(cat_ref.dtype)


_MIXER_PARAMS = ("win", "wout", "m_cw", "m_cb", "m_gbt", "m_nrm", "s_cw", "s_cb", "s_dtbt",
                 "s_alog", "s_alogt", "s_dskip", "s_nrm", "c_cw", "c_cb", "c_lg", "c_lb",
                 "tri", "trit", "expand")


P_PIECE = 256


def _project(hc, prm, p):
    for lo in range(0, P_COLS, P_PIECE):
        p[:, lo:lo + P_PIECE] = _dot(hc, prm["win"][:, lo:lo + P_PIECE])
        yield


def _round_robin(gens):
    gens = list(gens)
    while gens:
        for g in list(gens):
            try:
                next(g)
            except StopIteration:
                gens.remove(g)


def _mixer_kernel(x_ref, hb_ref, hn_ref, *refs):
    n_prm = len(_MIXER_PARAMS)
    prm = dict(zip(_MIXER_PARAMS, refs[:n_prm]))
    o_ref = refs[n_prm]
    extm_ref, exts_ref, extc_ref, c_ref, m_ref, h_ref, cat_ref, p_ref = refs[n_prm + 1:]
    tm = x_ref.shape[0]
    n_chunks = tm // CHUNK

    @pl.when((pl.program_id(0) == 0) & (pl.program_id(1) == 0))
    def _():
        for _ in _project(hb_ref[0:CHUNK, :], prm, p_ref.at[0]):
            pass

    @pl.when(pl.program_id(1) == 0)
    def _():
        extm_ref[0:CONV_CARRY, :] = jnp.zeros((CONV_CARRY, extm_ref.shape[1]), F32)
        exts_ref[0:CONV_CARRY, :] = jnp.zeros((CONV_CARRY, exts_ref.shape[1]), F32)
        extc_ref[0:C_CARRY, :] = jnp.zeros((C_CARRY, extc_ref.shape[1]), F32)
        c_ref[...] = jnp.zeros(c_ref.shape, F32)
        m_ref[...] = jnp.full(m_ref.shape, M_INIT_LOG, F32)
        h_ref[...] = jnp.zeros(h_ref.shape, F32)

    causal = _causal_mask()
    for c in range(n_chunks):
        rows = slice(c * CHUNK, (c + 1) * CHUNK)
        h_next = hb_ref[(c + 1) * CHUNK:(c + 2) * CHUNK, :] if c + 1 < n_chunks else hn_ref[...]
        p = p_ref.at[c % 2]
        proj = _project(h_next, prm, p_ref.at[(c + 1) % 2])
        _round_robin(
            [_mlstm_chunk(rows, p, prm, (extm_ref, c_ref, m_ref), cat_ref, causal), proj,
             _ssd_chunk(rows, p, prm, (exts_ref, h_ref), cat_ref, causal), proj,
             _convmod_chunk(rows, p, prm, extc_ref, cat_ref), proj])
        o_ref[rows, :] = x_ref[rows, :] + _dot(cat_ref[rows, :], prm["wout"][...])

    extm_ref[0:CONV_CARRY, :] = extm_ref[tm:tm + CONV_CARRY, :]
    exts_ref[0:CONV_CARRY, :] = exts_ref[tm:tm + CONV_CARRY, :]
    extc_ref[0:C_CARRY, :] = extc_ref[tm:tm + C_CARRY, :]


_MIXER_SHARED = ("tri", "trit", "expand")


def _mixer(x, hb, l, prm, batch, seq):
    d = x.shape[1]
    tm = TM_MIX
    nt = seq // tm
    params = [prm[k] for k in _MIXER_PARAMS]
    param_specs = [_resident(prm[k].shape) if k in _MIXER_SHARED else _layer(prm[k], l)
                   for k in _MIXER_PARAMS]
    n_chunks = tm // CHUNK
    assert n_chunks % 2 == 0 and seq % tm == 0
    last_chunk = batch * seq // CHUNK - 1
    next_chunk = lambda b, i: (jnp.minimum((b * nt + i + 1) * n_chunks, last_chunk), 0)
    return pl.pallas_call(
        _mixer_kernel,
        name="mixer",
        grid=(batch, nt),
        in_specs=[pl.BlockSpec((tm, d), lambda b, i: (b * nt + i, 0)),
                  pl.BlockSpec((tm, d), lambda b, i: (b * nt + i, 0)),
                  pl.BlockSpec((CHUNK, d), next_chunk)]
        + param_specs,
        out_specs=pl.BlockSpec((tm, d), lambda b, i: (b * nt + i, 0)),
        out_shape=jax.ShapeDtypeStruct((batch * seq, d), F32),
        scratch_shapes=[
            pltpu.VMEM((tm + CONV_CARRY, 2 * M_WIDTH), F32),
            pltpu.VMEM((tm + CONV_CARRY, XBC_WIDTH), F32),
            pltpu.VMEM((tm + C_CARRY, C_WIDTH), F32),
            pltpu.VMEM((M_HEADS, M_HEAD_DIM, 2 * M_HEAD_DIM), F32),
            pltpu.VMEM((M_HEADS, SUBLANES, LANES), F32),
            pltpu.VMEM((S_STATE, S_WIDTH), F32),
            pltpu.VMEM((tm, D_MIX), BF16),
            pltpu.VMEM((2, CHUNK, P_COLS), F32),
        ],
        compiler_params=pltpu.CompilerParams(
            dimension_semantics=("arbitrary", "arbitrary"), vmem_limit_bytes=VMEM_LIMIT),
    )(x, hb, hb, *params)


def _pad_lanes(a, width=LANES):
    return jnp.pad(a, [(0, 0)] * (a.ndim - 1) + [(0, width - a.shape[-1])])


def _rearrange_w_in(w):
    o_if = 4 * M_WIDTH
    o_z = o_if + 2 * M_HEADS
    o_xbc = o_z + S_WIDTH
    o_dt = o_xbc + XBC_WIDTH
    o_glu = o_dt + S_HEADS
    return jnp.concatenate([
        w[..., 0:o_if], w[..., o_xbc:o_dt], w[..., o_z:o_xbc], w[..., o_glu:],
        _pad_lanes(w[..., o_if:o_z]), _pad_lanes(w[..., o_dt:o_glu])], axis=-1).astype(BF16)


def kernel(x, ffn1_norm, ffn1_w_gate, ffn1_w_up, ffn1_w_down, mix_norm, w_in, w_out,
           mlstm_conv_w, mlstm_conv_b, mlstm_gate_b, mlstm_norm,
           ssd_conv_w, ssd_conv_b, ssd_dt_bias, ssd_a_log, ssd_d, ssd_norm,
           cm_conv_w, cm_conv_b, cm_ln_g, cm_ln_b,
           ffn2_norm, ffn2_w_gate, ffn2_w_up, ffn2_w_down, final_norm):
    batch, seq, d = x.shape
    depth = w_in.shape[0]
    xf = x.reshape(batch * seq, d)

    idx = jnp.arange(CHUNK)
    tri = (idx[None, :] <= idx[:, None]).astype(BF16)
    head_of_col = jnp.arange(S_WIDTH) // S_HEAD_DIM
    expand = (jnp.arange(LANES)[:, None] == head_of_col[None, :]).astype(BF16)
    rows = lambda a: a.reshape(depth, 1, -1).astype(F32)
    bf = lambda a: a.astype(BF16)
    lanes = lambda a: jnp.broadcast_to(a.astype(F32)[:, :, None], a.shape + (LANES,))
    fnorm = final_norm.reshape(1, -1).astype(F32)
    mix_g = rows(mix_norm)
    ffn1 = (rows(ffn1_norm), bf(ffn1_w_gate), bf(ffn1_w_up), bf(ffn1_w_down))
    ffn2 = (rows(ffn2_norm), bf(ffn2_w_gate), bf(ffn2_w_up), bf(ffn2_w_down))
    prm = dict(
        win=_rearrange_w_in(w_in), wout=bf(w_out),
        m_cw=mlstm_conv_w, m_cb=rows(mlstm_conv_b), m_gbt=lanes(mlstm_gate_b),
        m_nrm=rows(mlstm_norm),
        s_cw=ssd_conv_w, s_cb=rows(ssd_conv_b), s_dtbt=lanes(ssd_dt_bias),
        s_alog=_pad_lanes(rows(ssd_a_log)), s_alogt=lanes(ssd_a_log),
        s_dskip=rows(jnp.repeat(ssd_d, S_HEAD_DIM, axis=1)), s_nrm=rows(ssd_norm),
        c_cw=cm_conv_w, c_cb=rows(cm_conv_b), c_lg=rows(cm_ln_g), c_lb=rows(cm_ln_b),
        tri=tri, trit=tri.T, expand=expand)

    for l in range(depth):
        xf, hb = _ffn(xf, l, *ffn1, _layer(mix_g, l), mix_g, post="normed")
        xf = _mixer(xf, hb, l, prm, batch, seq)
        last = l == depth - 1
        xf = _ffn(xf, l, *ffn2, _resident(fnorm.shape), fnorm, post="final" if last else None)
    return xf.reshape(batch, seq, d)
```

```python
import functools

import jax
import jax.numpy as jnp
from jax import lax
from jax.experimental import pallas as pl
from jax.experimental.pallas import tpu as pltpu

F32 = jnp.float32
BF16 = jnp.bfloat16

RMS_EPS = 1e-6
LN_EPS = 1e-5
M_INIT_LOG = -1e30

CHUNK = 128
LANES = 128
SUBLANES = 8
M_HEADS = 4
M_HEAD_DIM = 128
M_WIDTH = M_HEADS * M_HEAD_DIM
M_CONV = 4
S_HEADS = 16
S_HEAD_DIM = 64
S_GROUPS = 4
S_STATE = 128
S_WIDTH = S_HEADS * S_HEAD_DIM
S_GROUP_WIDTH = S_WIDTH // S_GROUPS
S_HEADS_PER_GROUP = S_HEADS // S_GROUPS
S_CONV = 4
XBC_WIDTH = S_WIDTH + 2 * S_GROUPS * S_STATE
C_WIDTH = 512
C_KERNEL = 31
CONV_CARRY = SUBLANES
C_CARRY = 32
D_MIX = M_WIDTH + S_WIDTH + C_WIDTH

P_QKVO = 0
P_XBC = 2048
P_Z = 4096
P_GLU = 5120
P_GATES = 6144
P_COLS = 6400

TM_DENSE = 512
TM_MIX = 512
VMEM_LIMIT = 56 * 1024 * 1024


def _sigmoid(x):
    return 1.0 / (1.0 + jnp.exp(-x))


def _silu(x):
    return x * _sigmoid(x)


def _softplus(x):
    return jnp.maximum(x, 0.0) + jnp.log1p(jnp.exp(-jnp.abs(x)))


def _log_sigmoid(x):
    return jnp.minimum(x, 0.0) - jnp.log1p(jnp.exp(-jnp.abs(x)))


def _rms(x, g):
    return x * lax.rsqrt(jnp.mean(x * x, axis=-1, keepdims=True) + RMS_EPS) * g


def _dot(a, b):
    return jnp.dot(a, b, preferred_element_type=F32)


def _dot_nt(a, b):
    return lax.dot_general(a, b, (((1,), (1,)), ((), ())), preferred_element_type=F32)


def _dot_tn(a, b):
    return lax.dot_general(a, b, (((0,), (0,)), ((), ())), preferred_element_type=F32)


def _split3(a):
    hi = a.astype(BF16)
    r = a - hi.astype(F32)
    mid = r.astype(BF16)
    lo = (r - mid.astype(F32)).astype(BF16)
    return hi, mid, lo


def _dot_split_rhs(a_bf16, b_f32):
    hi, mid, lo = _split3(b_f32)
    return _dot(a_bf16, hi) + _dot(a_bf16, mid) + _dot(a_bf16, lo)


def _dot_split_lhs(a_f32, b_bf16):
    hi, mid, lo = _split3(a_f32)
    return _dot(hi, b_bf16) + _dot(mid, b_bf16) + _dot(lo, b_bf16)


def _expand(v, e):
    hi = v.astype(BF16)
    lo = (v - hi.astype(F32)).astype(BF16)
    return _dot(hi, e) + _dot(lo, e)


def _causal_mask():
    t = lax.broadcasted_iota(jnp.int32, (CHUNK, CHUNK), 0)
    s = lax.broadcasted_iota(jnp.int32, (CHUNK, CHUNK), 1)
    return s <= t


def _causal_conv(ext_ref, r0, carry, width, w_ref, b_ref, cols):
    win = ext_ref[r0:r0 + carry + CHUNK, cols]
    acc = b_ref[:, cols]
    if width == 4:
        w = [w_ref[j:j + 1, cols] for j in range(width)]
        win2 = pltpu.roll(win, 2, 0)
        even = w[3] * win[carry:, :] + w[1] * win2[carry:, :]
        odd = pltpu.roll(w[2] * win + w[0] * win2, 1, 0)[carry:, :]
        return (acc + even) + odd
    for k in range(SUBLANES):
        taps = [j for j in range(width) if (width - 1 - j) % SUBLANES == k]
        if not taps:
            continue
        rolled = win if k == 0 else pltpu.roll(win, k, 0)
        for j in taps:
            back = width - 1 - j
            start = carry - (back - k)
            acc = acc + w_ref[j:j + 1, cols] * rolled[start:start + CHUNK, :]
    return acc


def _ffn_kernel(x_ref, g_ref, wg_ref, wu_ref, wd_ref, ng_ref, o_ref, *maybe_h_ref, post):
    x = x_ref[...]
    h = _rms(x, g_ref[...]).astype(BF16)
    a = _dot(h, wg_ref[...])
    b = _dot(h, wu_ref[...])
    t = (_silu(a) * b).astype(BF16)
    y = x + 0.5 * _dot(t, wd_ref[...])
    if post == "final":
        y = _rms(y, ng_ref[...])
    o_ref[...] = y
    if post == "normed":
        maybe_h_ref[0][...] = _rms(y, ng_ref[...]).astype(BF16)


def _resident(shape):
    return pl.BlockSpec(shape, lambda *_: (0,) * len(shape), pipeline_mode=pl.Buffered(1))


def _layer(a, l):
    return pl.BlockSpec((None,) + a.shape[1:], lambda *_: (l,) + (0,) * (a.ndim - 1),
                        pipeline_mode=pl.Buffered(1))


def _ffn(x, l, g, wg, wu, wd, ng_spec, ng, post):
    t, d = x.shape
    tm = TM_DENSE
    tile = pl.BlockSpec((tm, d), lambda i: (i, 0))
    out_specs, out_shape = tile, jax.ShapeDtypeStruct((t, d), F32)
    if post == "normed":
        out_specs, out_shape = (tile, tile), (out_shape, jax.ShapeDtypeStruct((t, d), BF16))
    return pl.pallas_call(
        functools.partial(_ffn_kernel, post=post),
        name="ffn_" + str(post).lower(),
        grid=(t // tm,),
        in_specs=[tile, _layer(g, l), _layer(wg, l), _layer(wu, l), _layer(wd, l), ng_spec],
        out_specs=out_specs,
        out_shape=out_shape,
        compiler_params=pltpu.CompilerParams(
            dimension_semantics=("arbitrary",), vmem_limit_bytes=VMEM_LIMIT),
    )(x, g, wg, wu, wd, ng)


def _mlstm_chunk(rows, p, prm, scr, cat_ref, causal):
    extm_ref, c_ref, m_ref = scr
    r0 = rows.start
    extm_ref[CONV_CARRY + r0:CONV_CARRY + r0 + CHUNK, :] = p[:, P_QKVO:P_QKVO + 2 * M_WIDTH]
    conv = lambda cols: _silu(_causal_conv(extm_ref, r0, CONV_CARRY, M_CONV, prm["m_cw"], prm["m_cb"], cols))
    q_all = conv(slice(0, M_WIDTH))
    yield
    k_all = conv(slice(M_WIDTH, 2 * M_WIDTH)) * (M_HEAD_DIM ** -0.5)
    yield

    gg = p[:, P_GATES:P_GATES + LANES].T[0:2 * M_HEADS, :] + prm["m_gbt"][...]
    row = lax.broadcasted_iota(jnp.int32, gg.shape, 0)
    gt = jnp.where(row >= M_HEADS, _log_sigmoid(gg), gg)
    gc = jnp.concatenate([gt, jnp.zeros((LANES - 2 * M_HEADS, CHUNK), F32)], axis=0).T
    b_cols = _dot_split_rhs(prm["tri"][...], gc)
    b_rows = _dot_split_lhs(gt, prm["trit"][...])
    ones = jnp.ones((CHUNK, M_HEAD_DIM), BF16)
    yield

    for h in range(M_HEADS):
        cols = slice(h * M_HEAD_DIM, (h + 1) * M_HEAD_DIM)
        q_h = q_all[:, cols]
        k_h = k_all[:, cols]
        v_off = P_QKVO + 2 * M_WIDTH + h * M_HEAD_DIM
        o_off = P_QKVO + 3 * M_WIDTH + h * M_HEAD_DIM
        v_h = p[:, v_off:v_off + M_HEAD_DIM]
        o_h = p[:, o_off:o_off + M_HEAD_DIM]
        b_col = b_cols[:, M_HEADS + h:M_HEADS + h + 1]
        i_col = gc[:, h:h + 1]
        b_row = b_rows[M_HEADS + h:M_HEADS + h + 1, :]
        i_row = gt[h:h + 1, :]
        m = m_ref[h, 0:1, 0:1]
        g = b_col[CHUNK - 1:CHUNK, :]

        dmat = jnp.where(causal, (b_col - b_row) + i_row, -jnp.inf)
        inter = b_col + m
        m_t = jnp.maximum(inter, jnp.max(dmat, axis=1, keepdims=True))
        w = jnp.exp(dmat - m_t)
        sqk = (_dot_nt(q_h.astype(BF16), k_h.astype(BF16)) * w).astype(BF16)
        a = jnp.exp(inter - m_t)
        v_aug = jnp.concatenate([v_h.astype(BF16), ones], axis=1)
        c_aug = c_ref[h]
        lhs = jnp.concatenate([sqk, (a * q_h).astype(BF16)], axis=1)
        res = _dot(lhs, jnp.concatenate([v_aug, c_aug.astype(BF16)], axis=0))
        num = res[:, :M_HEAD_DIM]
        den = res[:, M_HEAD_DIM:M_HEAD_DIM + 1]
        hh = num / jnp.maximum(jnp.abs(den), jnp.exp(-m_t))

        u = (g - b_col) + i_col
        m_new = jnp.maximum(g + m, jnp.max(u, axis=0, keepdims=True))
        ws = jnp.exp(u - m_new)
        decay = jnp.exp(g + m - m_new)
        kw = (k_h * ws).astype(BF16)
        c_ref[h] = decay * c_aug + _dot_tn(kw, v_aug)
        m_ref[h] = jnp.broadcast_to(m_new, m_ref.shape[1:])

        out = _sigmoid(o_h) * hh
        cat_ref[rows, cols] = _rms(out, prm["m_nrm"][:, cols]).astype(cat_ref.dtype)
        yield


def _ssd_chunk(rows, p, prm, scr, cat_ref, causal):
    exts_ref, h_ref = scr
    r0 = rows.start
    exts_ref[CONV_CARRY + r0:CONV_CARRY + r0 + CHUNK, :] = p[:, P_XBC:P_XBC + XBC_WIDTH]
    conv = lambda off, n: _silu(_causal_conv(exts_ref, r0, CONV_CARRY, S_CONV, prm["s_cw"], prm["s_cb"],
                                             slice(off, off + n)))

    dt_t = _softplus(p[:, P_GATES + LANES:P_COLS].T[0:S_HEADS, :] + prm["s_dtbt"][...])
    adt_t = dt_t * (-jnp.exp(prm["s_alogt"][...]))
    dtc = jnp.concatenate([dt_t, jnp.zeros((LANES - S_HEADS, CHUNK), F32)], axis=0).T
    adt = dtc * (-jnp.exp(prm["s_alog"][...]))
    cs_col = _dot_split_rhs(prm["tri"][...], adt)
    cs_row = _dot_split_lhs(adt_t, prm["trit"][...])
    cs_last = cs_col[CHUNK - 1:CHUNK, :]
    ecs = jnp.exp(cs_col).astype(BF16)
    dcs = (jnp.exp(cs_last - cs_col) * dtc).astype(BF16)
    els = jnp.broadcast_to(jnp.exp(cs_last), (SUBLANES, LANES))
    seg = lax.broadcasted_iota(jnp.int32, (CHUNK, S_GROUP_WIDTH), 1) // S_HEAD_DIM
    yield

    for g in range(S_GROUPS):
        gcols = slice(g * S_GROUP_WIDTH, (g + 1) * S_GROUP_WIDTH)
        xs = conv(g * S_GROUP_WIDTH, S_GROUP_WIDTH)
        xs_b = xs.astype(BF16)
        b_g = conv(S_WIDTH + g * S_STATE, S_STATE).astype(BF16)
        c_g = conv(S_WIDTH + S_GROUPS * S_STATE + g * S_STATE, S_STATE).astype(BF16)
        yield
        cbm = _dot_nt(c_g, b_g)
        h_g = h_ref[:, gcols]
        e_g = prm["expand"][:, gcols]
        y_off = _dot(c_g, h_g.astype(BF16)) * _dot(ecs, e_g)
        ms, xst = [], []
        for r in range(S_HEADS_PER_GROUP):
            j = g * S_HEADS_PER_GROUP + r
            lmat = jnp.exp(jnp.where(causal, cs_col[:, j:j + 1] - cs_row[j:j + 1, :], -jnp.inf))
            ms.append((cbm * lmat * dt_t[j:j + 1, :]).astype(BF16))
            xst.append(jnp.where(seg == r, xs_b, jnp.zeros_like(xs_b)))
        y_diag = _dot(jnp.concatenate(ms, axis=1), jnp.concatenate(xst, axis=0))
        xd = (xs * _dot(dcs, e_g)).astype(BF16)
        h_ref[:, gcols] = _expand(els, e_g)[0:1, :] * h_g + _dot_tn(b_g, xd)

        y = y_diag + y_off + prm["s_dskip"][:, gcols] * xs
        y = y * _silu(p[:, P_Z + g * S_GROUP_WIDTH:P_Z + (g + 1) * S_GROUP_WIDTH])
        ocols = slice(M_WIDTH + g * S_GROUP_WIDTH, M_WIDTH + (g + 1) * S_GROUP_WIDTH)
        cat_ref[rows, ocols] = _rms(y, prm["s_nrm"][:, gcols]).astype(cat_ref.dtype)
        yield


def _convmod_chunk(rows, p, prm, extc_ref, cat_ref):
    r0 = rows.start
    gate = _sigmoid(p[:, P_GLU + C_WIDTH:P_GLU + 2 * C_WIDTH])
    extc_ref[C_CARRY + r0:C_CARRY + r0 + CHUNK, :] = p[:, P_GLU:P_GLU + C_WIDTH] * gate
    yield
    pieces = []
    for lo in range(0, C_WIDTH, LANES):
        pieces.append(_causal_conv(extc_ref, r0, C_CARRY, C_KERNEL, prm["c_cw"], prm["c_cb"],
                                   slice(lo, lo + LANES)))
        yield
    acc = jnp.concatenate(pieces, axis=1)
    mu = jnp.mean(acc, axis=-1, keepdims=True)
    xc = acc - mu
    y = xc * lax.rsqrt(jnp.mean(xc * xc, axis=-1, keepdims=True) + LN_EPS)
    y = y * prm["c_lg"][...] + prm["c_lb"][...]
    cat_ref[rows, M_WIDTH + S_WIDTH:] = _silu(y).astype(cat_ref.dtype)


_MIXER_PARAMS = ("win", "wout", "m_cw", "m_cb", "m_gbt", "m_nrm", "s_cw", "s_cb", "s_dtbt",
                 "s_alog", "s_alogt", "s_dskip", "s_nrm", "c_cw", "c_cb", "c_lg", "c_lb",
                 "tri", "trit", "expand")


P_PIECE = 256


def _project(hc, prm, p):
    for lo in range(0, P_COLS, P_PIECE):
        p[:, lo:lo + P_PIECE] = _dot(hc, prm["win"][:, lo:lo + P_PIECE])
        yield


def _round_robin(gens):
    gens = list(gens)
    while gens:
        for g in list(gens):
            try:
                next(g)
            except StopIteration:
                gens.remove(g)


def _mixer_kernel(x_ref, hb_ref, hn_ref, *refs):
    n_prm = len(_MIXER_PARAMS)
    prm = dict(zip(_MIXER_PARAMS, refs[:n_prm]))
    o_ref = refs[n_prm]
    extm_ref, exts_ref, extc_ref, c_ref, m_ref, h_ref, cat_ref, p_ref = refs[n_prm + 1:]
    tm = x_ref.shape[0]
    n_chunks = tm // CHUNK

    @pl.when((pl.program_id(0) == 0) & (pl.program_id(1) == 0))
    def _():
        for _ in _project(hb_ref[0:CHUNK, :], prm, p_ref.at[0]):
            pass

    @pl.when(pl.program_id(1) == 0)
    def _():
        extm_ref[0:CONV_CARRY, :] = jnp.zeros((CONV_CARRY, extm_ref.shape[1]), F32)
        exts_ref[0:CONV_CARRY, :] = jnp.zeros((CONV_CARRY, exts_ref.shape[1]), F32)
        extc_ref[0:C_CARRY, :] = jnp.zeros((C_CARRY, extc_ref.shape[1]), F32)
        c_ref[...] = jnp.zeros(c_ref.shape, F32)
        m_ref[...] = jnp.full(m_ref.shape, M_INIT_LOG, F32)
        h_ref[...] = jnp.zeros(h_ref.shape, F32)

    causal = _causal_mask()
    for c in range(n_chunks):
        rows = slice(c * CHUNK, (c + 1) * CHUNK)
        h_next = hb_ref[(c + 1) * CHUNK:(c + 2) * CHUNK, :] if c + 1 < n_chunks else hn_ref[...]
        p = p_ref.at[c % 2]
        proj = _project(h_next, prm, p_ref.at[(c + 1) % 2])
        _round_robin(
            [_mlstm_chunk(rows, p, prm, (extm_ref, c_ref, m_ref), cat_ref, causal), proj,
             _ssd_chunk(rows, p, prm, (exts_ref, h_ref), cat_ref, causal), proj,
             _convmod_chunk(rows, p, prm, extc_ref, cat_ref), proj])
        o_ref[rows, :] = x_ref[rows, :] + _dot(cat_ref[rows, :], prm["wout"][...])

    extm_ref[0:CONV_CARRY, :] = extm_ref[tm:tm + CONV_CARRY, :]
    exts_ref[0:CONV_CARRY, :] = exts_ref[tm:tm + CONV_CARRY, :]
    extc_ref[0:C_CARRY, :] = extc_ref[tm:tm + C_CARRY, :]


_MIXER_SHARED = ("tri", "trit", "expand")


def _mixer(x, hb, l, prm, batch, seq):
    d = x.shape[1]
    tm = TM_MIX
    nt = seq // tm
    params = [prm[k] for k in _MIXER_PARAMS]
    param_specs = [_resident(prm[k].shape) if k in _MIXER_SHARED else _layer(prm[k], l)
                   for k in _MIXER_PARAMS]
    n_chunks = tm // CHUNK
    assert n_chunks % 2 == 0 and seq % tm == 0
    last_chunk = batch * seq // CHUNK - 1
    next_chunk = lambda b, i: (jnp.minimum((b * nt + i + 1) * n_chunks, last_chunk), 0)
    return pl.pallas_call(
        _mixer_kernel,
        name="mixer",
        grid=(batch, nt),
        in_specs=[pl.BlockSpec((tm, d), lambda b, i: (b * nt + i, 0)),
                  pl.BlockSpec((tm, d), lambda b, i: (b * nt + i, 0)),
                  pl.BlockSpec((CHUNK, d), next_chunk)]
        + param_specs,
        out_specs=pl.BlockSpec((tm, d), lambda b, i: (b * nt + i, 0)),
        out_shape=jax.ShapeDtypeStruct((batch * seq, d), F32),
        scratch_shapes=[
            pltpu.VMEM((tm + CONV_CARRY, 2 * M_WIDTH), F32),
            pltpu.VMEM((tm + CONV_CARRY, XBC_WIDTH), F32),
            pltpu.VMEM((tm + C_CARRY, C_WIDTH), F32),
            pltpu.VMEM((M_HEADS, M_HEAD_DIM, 2 * M_HEAD_DIM), F32),
            pltpu.VMEM((M_HEADS, SUBLANES, LANES), F32),
            pltpu.VMEM((S_STATE, S_WIDTH), F32),
            pltpu.VMEM((tm, D_MIX), BF16),
            pltpu.VMEM((2, CHUNK, P_COLS), F32),
        ],
        compiler_params=pltpu.CompilerParams(
            dimension_semantics=("arbitrary", "arbitrary"), vmem_limit_bytes=VMEM_LIMIT),
    )(x, hb, hb, *params)


def _pad_lanes(a, width=LANES):
    return jnp.pad(a, [(0, 0)] * (a.ndim - 1) + [(0, width - a.shape[-1])])


def _rearrange_w_in(w):
    o_if = 4 * M_WIDTH
    o_z = o_if + 2 * M_HEADS
    o_xbc = o_z + S_WIDTH
    o_dt = o_xbc + XBC_WIDTH
    o_glu = o_dt + S_HEADS
    return jnp.concatenate([
        w[..., 0:o_if], w[..., o_xbc:o_dt], w[..., o_z:o_xbc], w[..., o_glu:],
        _pad_lanes(w[..., o_if:o_z]), _pad_lanes(w[..., o_dt:o_glu])], axis=-1).astype(BF16)


def kernel(x, ffn1_norm, ffn1_w_gate, ffn1_w_up, ffn1_w_down, mix_norm, w_in, w_out,
           mlstm_conv_w, mlstm_conv_b, mlstm_gate_b, mlstm_norm,
           ssd_conv_w, ssd_conv_b, ssd_dt_bias, ssd_a_log, ssd_d, ssd_norm,
           cm_conv_w, cm_conv_b, cm_ln_g, cm_ln_b,
           ffn2_norm, ffn2_w_gate, ffn2_w_up, ffn2_w_down, final_norm):
    batch, seq, d = x.shape
    depth = w_in.shape[0]
    xf = x.reshape(batch * seq, d)

    idx = jnp.arange(CHUNK)
    tri = (idx[None, :] <= idx[:, None]).astype(BF16)
    head_of_col = jnp.arange(S_WIDTH) // S_HEAD_DIM
    expand = (jnp.arange(LANES)[:, None] == head_of_col[None, :]).astype(BF16)
    rows = lambda a: a.reshape(depth, 1, -1).astype(F32)
    bf = lambda a: a.astype(BF16)
    lanes = lambda a: jnp.broadcast_to(a.astype(F32)[:, :, None], a.shape + (LANES,))
    fnorm = final_norm.reshape(1, -1).astype(F32)
    mix_g = rows(mix_norm)
    ffn1 = (rows(ffn1_norm), bf(ffn1_w_gate), bf(ffn1_w_up), bf(ffn1_w_down))
    ffn2 = (rows(ffn2_norm), bf(ffn2_w_gate), bf(ffn2_w_up), bf(ffn2_w_down))
    prm = dict(
        win=_rearrange_w_in(w_in), wout=bf(w_out),
        m_cw=mlstm_conv_w, m_cb=rows(mlstm_conv_b), m_gbt=lanes(mlstm_gate_b),
        m_nrm=rows(mlstm_norm),
        s_cw=ssd_conv_w, s_cb=rows(ssd_conv_b), s_dtbt=lanes(ssd_dt_bias),
        s_alog=_pad_lanes(rows(ssd_a_log)), s_alogt=lanes(ssd_a_log),
        s_dskip=rows(jnp.repeat(ssd_d, S_HEAD_DIM, axis=1)), s_nrm=rows(ssd_norm),
        c_cw=cm_conv_w, c_cb=rows(cm_conv_b), c_lg=rows(cm_ln_g), c_lb=rows(cm_ln_b),
        tri=tri, trit=tri.T, expand=expand)

    for l in range(depth):
        xf, hb = _ffn(xf, l, *ffn1, _layer(mix_g, l), mix_g, post="normed")
        xf = _mixer(xf, hb, l, prm, batch, seq)
        last = l == depth - 1
        xf = _ffn(xf, l, *ffn2, _resident(fnorm.shape), fnorm, post="final" if last else None)
    return xf.reshape(batch, seq, d)
```

```python
import functools

import jax
import jax.numpy as jnp
from jax import lax
from jax.experimental import pallas as pl
from jax.experimental.pallas import tpu as pltpu

F32 = jnp.float32
BF16 = jnp.bfloat16

RMS_EPS = 1e-6
LN_EPS = 1e-5
M_INIT_LOG = -1e30

CHUNK = 128
LANES = 128
SUBLANES = 8
M_HEADS = 4
M_HEAD_DIM = 128
M_WIDTH = M_HEADS * M_HEAD_DIM
M_CONV = 4
S_HEADS = 16
S_HEAD_DIM = 64
S_GROUPS = 4
S_STATE = 128
S_WIDTH = S_HEADS * S_HEAD_DIM
S_GROUP_WIDTH = S_WIDTH // S_GROUPS
S_HEADS_PER_GROUP = S_HEADS // S_GROUPS
S_CONV = 4
XBC_WIDTH = S_WIDTH + 2 * S_GROUPS * S_STATE
C_WIDTH = 512
C_KERNEL = 31
CONV_CARRY = SUBLANES
C_CARRY = 32
D_MIX = M_WIDTH + S_WIDTH + C_WIDTH

P_QKVO = 0
P_XBC = 2048
P_Z = 4096
P_GLU = 5120
P_GATES = 6144
P_COLS = 6400

TM_DENSE = 512
TM_MIX = 512
VMEM_LIMIT = 56 * 1024 * 1024


def _sigmoid(x):
    return 1.0 / (1.0 + jnp.exp(-x))


def _silu(x):
    return x * _sigmoid(x)


def _softplus(x):
    return jnp.maximum(x, 0.0) + jnp.log1p(jnp.exp(-jnp.abs(x)))


def _log_sigmoid(x):
    return jnp.minimum(x, 0.0) - jnp.log1p(jnp.exp(-jnp.abs(x)))


def _rms(x, g):
    return x * lax.rsqrt(jnp.mean(x * x, axis=-1, keepdims=True) + RMS_EPS) * g


def _dot(a, b):
    return jnp.dot(a, b, preferred_element_type=F32)


def _dot_nt(a, b):
    return lax.dot_general(a, b, (((1,), (1,)), ((), ())), preferred_element_type=F32)


def _dot_tn(a, b):
    return lax.dot_general(a, b, (((0,), (0,)), ((), ())), preferred_element_type=F32)


def _split3(a):
    hi = a.astype(BF16)
    r = a - hi.astype(F32)
    mid = r.astype(BF16)
    lo = (r - mid.astype(F32)).astype(BF16)
    return hi, mid, lo


def _dot_split_rhs(a_bf16, b_f32):
    hi, mid, lo = _split3(b_f32)
    return _dot(a_bf16, hi) + _dot(a_bf16, mid) + _dot(a_bf16, lo)


def _dot_split_lhs(a_f32, b_bf16):
    hi, mid, lo = _split3(a_f32)
    return _dot(hi, b_bf16) + _dot(mid, b_bf16) + _dot(lo, b_bf16)


def _expand(v, e):
    hi = v.astype(BF16)
    lo = (v - hi.astype(F32)).astype(BF16)
    return _dot(hi, e) + _dot(lo, e)


def _causal_mask():
    t = lax.broadcasted_iota(jnp.int32, (CHUNK, CHUNK), 0)
    s = lax.broadcasted_iota(jnp.int32, (CHUNK, CHUNK), 1)
    return s <= t


def _causal_conv(ext_ref, r0, carry, width, w_ref, b_ref, cols):
    win = ext_ref[r0:r0 + carry + CHUNK, cols]
    acc = b_ref[:, cols]
    if width == 4:
        w = [w_ref[j:j + 1, cols] for j in range(width)]
        win2 = pltpu.roll(win, 2, 0)
        even = w[3] * win[carry:, :] + w[1] * win2[carry:, :]
        odd = pltpu.roll(w[2] * win + w[0] * win2, 1, 0)[carry:, :]
        return (acc + even) + odd
    for k in range(SUBLANES):
        taps = [j for j in range(width) if (width - 1 - j) % SUBLANES == k]
        if not taps:
            continue
        rolled = win if k == 0 else pltpu.roll(win, k, 0)
        for j in taps:
            back = width - 1 - j
            start = carry - (back - k)
            acc = acc + w_ref[j:j + 1, cols] * rolled[start:start + CHUNK, :]
    return acc


def _ffn_kernel(x_ref, g_ref, wg_ref, wu_ref, wd_ref, ng_ref, o_ref, *maybe_h_ref, post):
    x = x_ref[...]
    h = _rms(x, g_ref[...]).astype(BF16)
    a = _dot(h, wg_ref[...])
    b = _dot(h, wu_ref[...])
    t = (_silu(a) * b).astype(BF16)
    y = x + 0.5 * _dot(t, wd_ref[...])
    if post == "final":
        y = _rms(y, ng_ref[...])
    o_ref[...] = y
    if post == "normed":
        maybe_h_ref[0][...] = _rms(y, ng_ref[...]).astype(BF16)


def _resident(shape):
    return pl.BlockSpec(shape, lambda *_: (0,) * len(shape), pipeline_mode=pl.Buffered(1))


def _layer(a, l):
    return pl.BlockSpec((None,) + a.shape[1:], lambda *_: (l,) + (0,) * (a.ndim - 1),
                        pipeline_mode=pl.Buffered(1))


def _ffn(x, l, g, wg, wu, wd, ng_spec, ng, post):
    t, d = x.shape
    tm = TM_DENSE
    tile = pl.BlockSpec((tm, d), lambda i: (i, 0))
    out_specs, out_shape = tile, jax.ShapeDtypeStruct((t, d), F32)
    if post == "normed":
        out_specs, out_shape = (tile, tile), (out_shape, jax.ShapeDtypeStruct((t, d), BF16))
    return pl.pallas_call(
        functools.partial(_ffn_kernel, post=post),
        name="ffn_" + str(post).lower(),
        grid=(t // tm,),
        in_specs=[tile, _layer(g, l), _layer(wg, l), _layer(wu, l), _layer(wd, l), ng_spec],
        out_specs=out_specs,
        out_shape=out_shape,
        compiler_params=pltpu.CompilerParams(
            dimension_semantics=("arbitrary",), vmem_limit_bytes=VMEM_LIMIT),
    )(x, g, wg, wu, wd, ng)


def _mlstm_chunk(rows, p, prm, scr, cat_ref, causal):
    extm_ref, c_ref, m_ref = scr
    r0 = rows.start
    extm_ref[CONV_CARRY + r0:CONV_CARRY + r0 + CHUNK, :] = p[:, P_QKVO:P_QKVO + 2 * M_WIDTH]
    conv = lambda cols: _silu(_causal_conv(extm_ref, r0, CONV_CARRY, M_CONV, prm["m_cw"], prm["m_cb"], cols))

    gg = p[:, P_GATES:P_GATES + LANES].T[0:2 * M_HEADS, :] + prm["m_gbt"][...]
    row = lax.broadcasted_iota(jnp.int32, gg.shape, 0)
    gt = jnp.where(row >= M_HEADS, _log_sigmoid(gg), gg)
    gc = jnp.concatenate([gt, jnp.zeros((LANES - 2 * M_HEADS, CHUNK), F32)], axis=0).T
    b_cols = _dot_split_rhs(prm["tri"][...], gc)
    b_rows = _dot_split_lhs(gt, prm["trit"][...])
    ones = jnp.ones((CHUNK, M_HEAD_DIM), BF16)
    yield

    for h in range(M_HEADS):
        cols = slice(h * M_HEAD_DIM, (h + 1) * M_HEAD_DIM)
        q_h = conv(cols)
        k_h = conv(slice(M_WIDTH + cols.start, M_WIDTH + cols.stop)) * (M_HEAD_DIM ** -0.5)
        yield
        v_off = P_QKVO + 2 * M_WIDTH + h * M_HEAD_DIM
        o_off = P_QKVO + 3 * M_WIDTH + h * M_HEAD_DIM
        v_h = p[:, v_off:v_off + M_HEAD_DIM]
        o_h = p[:, o_off:o_off + M_HEAD_DIM]
        b_col = b_cols[:, M_HEADS + h:M_HEADS + h + 1]
        i_col = gc[:, h:h + 1]
        b_row = b_rows[M_HEADS + h:M_HEADS + h + 1, :]
        i_row = gt[h:h + 1, :]
        m = m_ref[h, 0:1, 0:1]
        g = b_col[CHUNK - 1:CHUNK, :]

        dmat = jnp.where(causal, (b_col - b_row) + i_row, -jnp.inf)
        inter = b_col + m
        m_t = jnp.maximum(inter, jnp.max(dmat, axis=1, keepdims=True))
        w = jnp.exp(dmat - m_t)
        sqk = (_dot_nt(q_h.astype(BF16), k_h.astype(BF16)) * w).astype(BF16)
        a = jnp.exp(inter - m_t)
        v_aug = jnp.concatenate([v_h.astype(BF16), ones], axis=1)
        c_aug = c_ref[h]
        lhs = jnp.concatenate([sqk, (a * q_h).astype(BF16)], axis=1)
        res = _dot(lhs, jnp.concatenate([v_aug, c_aug.astype(BF16)], axis=0))
        num = res[:, :M_HEAD_DIM]
        den = res[:, M_HEAD_DIM:M_HEAD_DIM + 1]
        hh = num / jnp.maximum(jnp.abs(den), jnp.exp(-m_t))

        u = (g - b_col) + i_col
        m_new = jnp.maximum(g + m, jnp.max(u, axis=0, keepdims=True))
        ws = jnp.exp(u - m_new)
        decay = jnp.exp(g + m - m_new)
        kw = (k_h * ws).astype(BF16)
        c_ref[h] = decay * c_aug + _dot_tn(kw, v_aug)
        m_ref[h] = jnp.broadcast_to(m_new, m_ref.shape[1:])

        out = _sigmoid(o_h) * hh
        cat_ref[rows, cols] = _rms(out, prm["m_nrm"][:, cols]).astype(cat_ref.dtype)
        yield


def _ssd_chunk(rows, p, prm, scr, cat_ref, causal):
    exts_ref, h_ref = scr
    r0 = rows.start
    exts_ref[CONV_CARRY + r0:CONV_CARRY + r0 + CHUNK, :] = p[:, P_XBC:P_XBC + XBC_WIDTH]
    conv = lambda off, n: _silu(_causal_conv(exts_ref, r0, CONV_CARRY, S_CONV, prm["s_cw"], prm["s_cb"],
                                             slice(off, off + n)))

    dt_t = _softplus(p[:, P_GATES + LANES:P_COLS].T[0:S_HEADS, :] + prm["s_dtbt"][...])
    adt_t = dt_t * (-jnp.exp(prm["s_alogt"][...]))
    dtc = jnp.concatenate([dt_t, jnp.zeros((LANES - S_HEADS, CHUNK), F32)], axis=0).T
    adt = dtc * (-jnp.exp(prm["s_alog"][...]))
    cs_col = _dot_split_rhs(prm["tri"][...], adt)
    cs_row = _dot_split_lhs(adt_t, prm["trit"][...])
    cs_last = cs_col[CHUNK - 1:CHUNK, :]
    ecs = jnp.exp(cs_col).astype(BF16)
    dcs = (jnp.exp(cs_last - cs_col) * dtc).astype(BF16)
    els = jnp.broadcast_to(jnp.exp(cs_last), (SUBLANES, LANES))
    seg = lax.broadcasted_iota(jnp.int32, (CHUNK, S_GROUP_WIDTH), 1) // S_HEAD_DIM
    yield

    for g in range(S_GROUPS):
        gcols = slice(g * S_GROUP_WIDTH, (g + 1) * S_GROUP_WIDTH)
        xs = conv(g * S_GROUP_WIDTH, S_GROUP_WIDTH)
        xs_b = xs.astype(BF16)
        b_g = conv(S_WIDTH + g * S_STATE, S_STATE).astype(BF16)
        c_g = conv(S_WIDTH + S_GROUPS * S_STATE + g * S_STATE, S_STATE).astype(BF16)
        yield
        cbm = _dot_nt(c_g, b_g)
        h_g = h_ref[:, gcols]
        e_g = prm["expand"][:, gcols]
        y_off = _dot(c_g, h_g.astype(BF16)) * _dot(ecs, e_g)
        ms, xst = [], []
        for r in range(S_HEADS_PER_GROUP):
            j = g * S_HEADS_PER_GROUP + r
            lmat = jnp.exp(jnp.where(causal, cs_col[:, j:j + 1] - cs_row[j:j + 1, :], -jnp.inf))
            ms.append((cbm * lmat * dt_t[j:j + 1, :]).astype(BF16))
            xst.append(jnp.where(seg == r, xs_b, jnp.zeros_like(xs_b)))
        y_diag = _dot(jnp.concatenate(ms, axis=1), jnp.concatenate(xst, axis=0))
        xd = (xs * _dot(dcs, e_g)).astype(BF16)
        h_ref[:, gcols] = _expand(els, e_g)[0:1, :] * h_g + _dot_tn(b_g, xd)

        y = y_diag + y_off + prm["s_dskip"][:, gcols] * xs
        y = y * _silu(p[:, P_Z + g * S_GROUP_WIDTH:P_Z + (g + 1) * S_GROUP_WIDTH])
        ocols = slice(M_WIDTH + g * S_GROUP_WIDTH, M_WIDTH + (g + 1) * S_GROUP_WIDTH)
        cat_ref[rows, ocols] = _rms(y, prm["s_nrm"][:, gcols]).astype(cat_ref.dtype)
        yield


def _convmod_chunk(rows, p, prm, extc_ref, cat_ref):
    r0 = rows.start
    gate = _sigmoid(p[:, P_GLU + C_WIDTH:P_GLU + 2 * C_WIDTH])
    extc_ref[C_CARRY + r0:C_CARRY + r0 + CHUNK, :] = p[:, P_GLU:P_GLU + C_WIDTH] * gate
    yield
    pieces = []
    for lo in range(0, C_WIDTH, LANES):
        pieces.append(_causal_conv(extc_ref, r0, C_CARRY, C_KERNEL, prm["c_cw"], prm["c_cb"],
                                   slice(lo, lo + LANES)))
        yield
    acc = jnp.concatenate(pieces, axis=1)
    mu = jnp.mean(acc, axis=-1, keepdims=True)
    xc = acc - mu
    y = xc * lax.rsqrt(jnp.mean(xc * xc, axis=-1, keepdims=True) + LN_EPS)
    y = y * prm["c_lg"][...] + prm["c_lb"][...]
    cat_ref[rows, M_WIDTH + S_WIDTH:] = _silu(y).astype(cat_ref.dtype)


_MIXER_PARAMS = ("win", "wout", "m_cw", "m_cb", "m_gbt", "m_nrm", "s_cw", "s_cb", "s_dtbt",
                 "s_alog", "s_alogt", "s_dskip", "s_nrm", "c_cw", "c_cb", "c_lg", "c_lb",
                 "tri", "trit", "expand")


P_PIECE = 256


def _project(hc, prm, p):
    for lo in range(0, P_COLS, P_PIECE):
        p[:, lo:lo + P_PIECE] = _dot(hc, prm["win"][:, lo:lo + P_PIECE])
        yield


def _round_robin(gens):
    gens = list(gens)
    while gens:
        for g in list(gens):
            try:
                next(g)
            except StopIteration:
                gens.remove(g)


def _mixer_kernel(x_ref, hb_ref, hn_ref, *refs):
    n_prm = len(_MIXER_PARAMS)
    prm = dict(zip(_MIXER_PARAMS, refs[:n_prm]))
    o_ref = refs[n_prm]
    extm_ref, exts_ref, extc_ref, c_ref, m_ref, h_ref, cat_ref, p_ref = refs[n_prm + 1:]
    tm = x_ref.shape[0]
    n_chunks = tm // CHUNK

    @pl.when((pl.program_id(0) == 0) & (pl.program_id(1) == 0))
    def _():
        for _ in _project(hb_ref[0:CHUNK, :], prm, p_ref.at[0]):
            pass

    @pl.when(pl.program_id(1) == 0)
    def _():
        extm_ref[0:CONV_CARRY, :] = jnp.zeros((CONV_CARRY, extm_ref.shape[1]), F32)
        exts_ref[0:CONV_CARRY, :] = jnp.zeros((CONV_CARRY, exts_ref.shape[1]), F32)
        extc_ref[0:C_CARRY, :] = jnp.zeros((C_CARRY, extc_ref.shape[1]), F32)
        c_ref[...] = jnp.zeros(c_ref.shape, F32)
        m_ref[...] = jnp.full(m_ref.shape, M_INIT_LOG, F32)
        h_ref[...] = jnp.zeros(h_ref.shape, F32)

    causal = _causal_mask()
    for c in range(n_chunks):
        rows = slice(c * CHUNK, (c + 1) * CHUNK)
        h_next = hb_ref[(c + 1) * CHUNK:(c + 2) * CHUNK, :] if c + 1 < n_chunks else hn_ref[...]
        p = p_ref.at[c % 2]
        proj = _project(h_next, prm, p_ref.at[(c + 1) % 2])
        _round_robin(
            [_mlstm_chunk(rows, p, prm, (extm_ref, c_ref, m_ref), cat_ref, causal), proj,
             _ssd_chunk(rows, p, prm, (exts_ref, h_ref), cat_ref, causal), proj,
             _convmod_chunk(rows, p, prm, extc_ref, cat_ref), proj])
        o_ref[rows, :] = x_ref[rows, :] + _dot(cat_ref[rows, :], prm["wout"][...])

    extm_ref[0:CONV_CARRY, :] = extm_ref[tm:tm + CONV_CARRY, :]
    exts_ref[0:CONV_CARRY, :] = exts_ref[tm:tm + CONV_CARRY, :]
    extc_ref[0:C_CARRY, :] = extc_ref[tm:tm + C_CARRY, :]


_MIXER_SHARED = ("tri", "trit", "expand")


def _mixer(x, hb, l, prm, batch, seq):
    d = x.shape[1]
    tm = TM_MIX
    nt = seq // tm
    params = [prm[k] for k in _MIXER_PARAMS]
    param_specs = [_resident(prm[k].shape) if k in _MIXER_SHARED else _layer(prm[k], l)
                   for k in _MIXER_PARAMS]
    n_chunks = tm // CHUNK
    assert n_chunks % 2 == 0 and seq % tm == 0
    last_chunk = batch * seq // CHUNK - 1
    next_chunk = lambda b, i: (jnp.minimum((b * nt + i + 1) * n_chunks, last_chunk), 0)
    return pl.pallas_call(
        _mixer_kernel,
        name="mixer",
        grid=(batch, nt),
        in_specs=[pl.BlockSpec((tm, d), lambda b, i: (b * nt + i, 0)),
                  pl.BlockSpec((tm, d), lambda b, i: (b * nt + i, 0)),
                  pl.BlockSpec((CHUNK, d), next_chunk)]
        + param_specs,
        out_specs=pl.BlockSpec((tm, d), lambda b, i: (b * nt + i, 0)),
        out_shape=jax.ShapeDtypeStruct((batch * seq, d), F32),
        scratch_shapes=[
            pltpu.VMEM((tm + CONV_CARRY, 2 * M_WIDTH), F32),
            pltpu.VMEM((tm + CONV_CARRY, XBC_WIDTH), F32),
            pltpu.VMEM((tm + C_CARRY, C_WIDTH), F32),
            pltpu.VMEM((M_HEADS, M_HEAD_DIM, 2 * M_HEAD_DIM), F32),
            pltpu.VMEM((M_HEADS, SUBLANES, LANES), F32),
            pltpu.VMEM((S_STATE, S_WIDTH), F32),
            pltpu.VMEM((tm, D_MIX), BF16),
            pltpu.VMEM((2, CHUNK, P_COLS), F32),
        ],
        compiler_params=pltpu.CompilerParams(
            dimension_semantics=("arbitrary", "arbitrary"), vmem_limit_bytes=VMEM_LIMIT),
    )(x, hb, hb, *params)


def _pad_lanes(a, width=LANES):
    return jnp.pad(a, [(0, 0)] * (a.ndim - 1) + [(0, width - a.shape[-1])])


def _rearrange_w_in(w):
    o_if = 4 * M_WIDTH
    o_z = o_if + 2 * M_HEADS
    o_xbc = o_z + S_WIDTH
    o_dt = o_xbc + XBC_WIDTH
    o_glu = o_dt + S_HEADS
    return jnp.concatenate([
        w[..., 0:o_if], w[..., o_xbc:o_dt], w[..., o_z:o_xbc], w[..., o_glu:],
        _pad_lanes(w[..., o_if:o_z]), _pad_lanes(w[..., o_dt:o_glu])], axis=-1).astype(BF16)


def kernel(x, ffn1_norm, ffn1_w_gate, ffn1_w_up, ffn1_w_down, mix_norm, w_in, w_out,
           mlstm_conv_w, mlstm_conv_b, mlstm_gate_b, mlstm_norm,
           ssd_conv_w, ssd_conv_b, ssd_dt_bias, ssd_a_log, ssd_d, ssd_norm,
           cm_conv_w, cm_conv_b, cm_ln_g, cm_ln_b,
           ffn2_norm, ffn2_w_gate, ffn2_w_up, ffn2_w_down, final_norm):
    batch, seq, d = x.shape
    depth = w_in.shape[0]
    xf = x.reshape(batch * seq, d)

    idx = jnp.arange(CHUNK)
    tri = (idx[None, :] <= idx[:, None]).astype(BF16)
    head_of_col = jnp.arange(S_WIDTH) // S_HEAD_DIM
    expand = (jnp.arange(LANES)[:, None] == head_of_col[None, :]).astype(BF16)
    rows = lambda a: a.reshape(depth, 1, -1).astype(F32)
    bf = lambda a: a.astype(BF16)
    lanes = lambda a: jnp.broadcast_to(a.astype(F32)[:, :, None], a.shape + (LANES,))
    fnorm = final_norm.reshape(1, -1).astype(F32)
    mix_g = rows(mix_norm)
    ffn1 = (rows(ffn1_norm), bf(ffn1_w_gate), bf(ffn1_w_up), bf(ffn1_w_down))
    ffn2 = (rows(ffn2_norm), bf(ffn2_w_gate), bf(ffn2_w_up), bf(ffn2_w_down))
    prm = dict(
        win=_rearrange_w_in(w_in), wout=bf(w_out),
        m_cw=mlstm_conv_w, m_cb=rows(mlstm_conv_b), m_gbt=lanes(mlstm_gate_b),
        m_nrm=rows(mlstm_norm),
        s_cw=ssd_conv_w, s_cb=rows(ssd_conv_b), s_dtbt=lanes(ssd_dt_bias),
        s_alog=_pad_lanes(rows(ssd_a_log)), s_alogt=lanes(ssd_a_log),
        s_dskip=rows(jnp.repeat(ssd_d, S_HEAD_DIM, axis=1)), s_nrm=rows(ssd_norm),
        c_cw=cm_conv_w, c_cb=rows(cm_conv_b), c_lg=rows(cm_ln_g), c_lb=rows(cm_ln_b),
        tri=tri, trit=tri.T, expand=expand)

    for l in range(depth):
        xf, hb = _ffn(xf, l, *ffn1, _layer(mix_g, l), mix_g, post="normed")
        xf = _mixer(xf, hb, l, prm, batch, seq)
        last = l == depth - 1
        xf = _ffn(xf, l, *ffn2, _resident(fnorm.shape), fnorm, post="final" if last else None)
    return xf.reshape(batch, seq, d)
```
